```python
import jax, jax.numpy as jnp
from jax import lax
import numpy as np

D_MODEL = 2048
BATCH = 1
SEQ = 8192
DEPTH = 2

N_A_LAYERS = DEPTH // 2
N_B_LAYERS = DEPTH - N_A_LAYERS
RMS_EPS = 1e-6
GATED_NORM_EPS = 1e-5

SSM_EXPAND = 2
D_INNER = SSM_EXPAND * D_MODEL
SSM_HEAD_DIM = 64
SSM_HEADS = D_INNER // SSM_HEAD_DIM
SSM_GROUPS = 8
SSM_HEADS_PER_GROUP = SSM_HEADS // SSM_GROUPS
SSM_STATE = 128
D_CONV = 4
CONV_DIM = D_INNER + 2 * SSM_GROUPS * SSM_STATE
D_IN_PROJ = 2 * D_INNER + 2 * SSM_GROUPS * SSM_STATE + SSM_HEADS
SSD_CHUNK = 256

FOX_HEADS = 16
FOX_HEAD_DIM = D_MODEL // FOX_HEADS
ATT_DIM = FOX_HEADS * FOX_HEAD_DIM
FOX_Q_BLOCK = 128
FOX_SCALE = FOX_HEAD_DIM ** -0.5

PEER_HEADS = 8
PEER_N_KEYS = 128
PEER_EXPERTS = PEER_N_KEYS * PEER_N_KEYS
PEER_TOPK = 16
PEER_QUERY_DIM = 256
PEER_HALF = PEER_QUERY_DIM // 2
PEER_TOKEN_BLOCK = 128

PLE_DIM = 256

kernel_name = 'yoco_ssd_fox_peer_ple_trunk'


def rms_norm(x, w, eps=RMS_EPS):
    xf = x.astype(jnp.float32)
    y = xf * lax.rsqrt(jnp.mean(xf * xf, axis=-1, keepdims=True) + eps)
    return y.astype(x.dtype) * w


def causal_depthwise_conv(x, w, b):
    s = x.shape[1]
    xp = jnp.pad(x, ((0, 0), (D_CONV - 1, 0), (0, 0)))
    y = b + w[0] * xp[:, 0:s]
    for k in range(1, D_CONV):
        y = y + w[k] * xp[:, k:k + s]
    return y


def ssd_chunked_scan(xh, dt, A, Bm, Cm):
    b, s, g, r, p = xh.shape
    n = Bm.shape[-1]
    pad = (-s) % SSD_CHUNK
    if pad:
        padf = lambda t: jnp.pad(t, ((0, 0), (0, pad)) + ((0, 0),) * (t.ndim - 2))
        xh, dt, Bm, Cm = padf(xh), padf(dt), padf(Bm), padf(Cm)
    nc = (s + pad) // SSD_CHUNK
    chunks = lambda t: jnp.moveaxis(t.reshape((b, nc, SSD_CHUNK) + t.shape[2:]), 1, 0)
    xc, dtc, Bc, Cc = chunks(xh), chunks(dt), chunks(Bm), chunks(Cm)
    acum = jnp.cumsum(dtc * A, axis=2)
    tril = jnp.tril(jnp.ones((SSD_CHUNK, SSD_CHUNK), bool))[None, :, :, None, None]

    def step(state, inp):
        x, dtl, ac, Bl, Cl = inp
        seg = ac[:, :, None] - ac[:, None, :]
        decay = jnp.exp(jnp.where(tril, seg, -jnp.inf))
        cb = jnp.einsum('blgn,bsgn->blsg', Cl, Bl)
        y_intra = jnp.einsum('blsg,blsgr,bsgr,bsgrp->blgrp', cb, decay, dtl, x)
        y_inter = jnp.einsum('blgn,bgrpn,blgr->blgrp', Cl, state, jnp.exp(ac))
        last = ac[:, -1]
        w = jnp.exp(last[:, None] - ac) * dtl
        new_state = state * jnp.exp(last)[..., None, None] + jnp.einsum('blgr,blgrp,blgn->bgrpn', w, x, Bl)
        return new_state, y_intra + y_inter

    init = jnp.zeros((b, g, r, p, n), jnp.float32)
    _, ys = lax.scan(step, init, (xc, dtc, acum, Bc, Cc))
    return jnp.moveaxis(ys, 0, 1).reshape(b, nc * SSD_CHUNK, g, r, p)[:, :s]


def mamba2_mixer(h, norm_w, w_in, conv_w, conv_b, dt_bias, A_log, D_skip, gnorm_w, w_out):
    b, s, _ = h.shape
    zxbcdt = rms_norm(h, norm_w) @ w_in
    z = zxbcdt[..., :D_INNER]
    xbc = zxbcdt[..., D_INNER:D_INNER + CONV_DIM]
    dt_raw = zxbcdt[..., D_INNER + CONV_DIM:]
    xbc = jax.nn.silu(causal_depthwise_conv(xbc, conv_w, conv_b))
    gn = SSM_GROUPS * SSM_STATE
    xs = xbc[..., :D_INNER].reshape(b, s, SSM_GROUPS, SSM_HEADS_PER_GROUP, SSM_HEAD_DIM)
    Bm = xbc[..., D_INNER:D_INNER + gn].reshape(b, s, SSM_GROUPS, SSM_STATE)
    Cm = xbc[..., D_INNER + gn:].reshape(b, s, SSM_GROUPS, SSM_STATE)
    dt = jax.nn.softplus(dt_raw.astype(jnp.float32) + dt_bias.astype(jnp.float32))
    dt = dt.reshape(b, s, SSM_GROUPS, SSM_HEADS_PER_GROUP)
    A = -jnp.exp(A_log.astype(jnp.float32)).reshape(SSM_GROUPS, SSM_HEADS_PER_GROUP)
    y = ssd_chunked_scan(xs, dt, A, Bm, Cm)
    y = y + D_skip.reshape(SSM_GROUPS, SSM_HEADS_PER_GROUP)[..., None] * xs
    y = y.reshape(b, s, D_INNER).astype(h.dtype)
    gf = (y * jax.nn.silu(z)).astype(jnp.float32).reshape(b, s, SSM_GROUPS, D_INNER // SSM_GROUPS)
    gf = gf * lax.rsqrt(jnp.mean(gf * gf, axis=-1, keepdims=True) + GATED_NORM_EPS)
    y = gf.reshape(b, s, D_INNER).astype(h.dtype) * gnorm_w
    return y @ w_out


def fox_shared_kv(h, kv_norm, w_kvf, b_f):
    b, s, _ = h.shape
    kvf = rms_norm(h, kv_norm) @ w_kvf
    k = kvf[..., :ATT_DIM].reshape(b, s, FOX_HEADS, FOX_HEAD_DIM)
    v = kvf[..., ATT_DIM:2 * ATT_DIM].reshape(b, s, FOX_HEADS, FOX_HEAD_DIM)
    log_f = jax.nn.log_sigmoid(kvf[..., 2 * ATT_DIM:].astype(jnp.float32) + b_f.astype(jnp.float32))
    cum = jnp.cumsum(log_f, axis=1)
    return k, v, cum


def fox_attention(h, norm_w, w_q, w_o, k, v, cum):
    b, s, _ = h.shape
    nb = s // FOX_Q_BLOCK
    q = (rms_norm(h, norm_w) @ w_q).reshape(b, nb, FOX_Q_BLOCK, FOX_HEADS, FOX_HEAD_DIM)
    q = jnp.moveaxis(q, 1, 0)
    cq = jnp.moveaxis(cum.reshape(b, nb, FOX_Q_BLOCK, FOX_HEADS), 1, 0)
    qpos = jnp.arange(s, dtype=jnp.int32).reshape(nb, FOX_Q_BLOCK)
    kpos = jnp.arange(s, dtype=jnp.int32)
    ck = jnp.transpose(cum, (0, 2, 1))[:, :, None, :]

    def block(args):
        qi, ci, pi = args
        logits = jnp.einsum('bqhd,bkhd->bhqk', qi, k).astype(jnp.float32) * FOX_SCALE
        logits = logits + jnp.transpose(ci, (0, 2, 1))[..., None] - ck
        logits = jnp.where(kpos[None, :] <= pi[:, None], logits, -jnp.inf)
        probs = jax.nn.softmax(logits, axis=-1).astype(v.dtype)
        return jnp.einsum('bhqk,bkhd->bqhd', probs, v)

    o = lax.map(block, (q, cq, qpos))
    o = jnp.moveaxis(o, 0, 1).reshape(b, s, ATT_DIM)
    return o @ w_o


def peer_ffn(h, norm_w, w_q, sub_keys, u_tab, v_tab):
    b, s, d = h.shape
    xn = rms_norm(h, norm_w)
    q = (xn @ w_q).reshape(b, s, PEER_HEADS, 2, PEER_HALF)
    sc = jnp.einsum('bshcd,hckd->bshck', q, sub_keys).astype(jnp.float32)
    s1, i1 = lax.top_k(sc[..., 0, :], PEER_TOPK)
    s2, i2 = lax.top_k(sc[..., 1, :], PEER_TOPK)
    cand = (s1[..., :, None] + s2[..., None, :]).reshape(b, s, PEER_HEADS, PEER_TOPK * PEER_TOPK)
    cidx = (i1[..., :, None] * PEER_N_KEYS + i2[..., None, :]).reshape(b, s, PEER_HEADS, PEER_TOPK * PEER_TOPK)
    top, sel = lax.top_k(cand, PEER_TOPK)
    eidx = jnp.take_along_axis(cidx, sel, axis=-1)
    gates = jax.nn.softmax(top, axis=-1).astype(h.dtype)
    nblk = (b * s) // PEER_TOKEN_BLOCK
    n_sel = PEER_HEADS * PEER_TOPK
    xb = xn.reshape(nblk, PEER_TOKEN_BLOCK, d)
    ib = eidx.reshape(nblk, PEER_TOKEN_BLOCK, n_sel)
    gb = gates.reshape(nblk, PEER_TOKEN_BLOCK, n_sel)

    def block(args):
        xt, it, gt = args
        act = jax.nn.gelu(jnp.einsum('td,ted->te', xt, u_tab[it]), approximate=False)
        return jnp.einsum('te,ted->td', gt * act, v_tab[it])

    y = lax.map(block, (xb, ib, gb))
    return y.reshape(b, s, d)


def per_layer_embedding(h, p_i, norm_w, w_gate, b_gate, w_proj):
    gate = jax.nn.sigmoid(rms_norm(h, norm_w) @ w_gate + b_gate)
    return gate * (p_i @ w_proj)


def setup_inputs(seed: int = 0) -> dict:
    key = jax.random.key(seed)
    ks = jax.random.split(key, 28)
    f32 = jnp.float32
    nrm = lambda k, shape, scale: jax.random.normal(k, shape, f32) * scale
    gain = lambda k, shape: 1.0 + 0.02 * jax.random.normal(k, shape, f32)
    dt0 = jnp.exp(jax.random.uniform(ks[6], (N_A_LAYERS, SSM_HEADS), f32, np.log(1e-3), np.log(1e-1)))
    return {
        'x': nrm(ks[0], (BATCH, SEQ, D_MODEL), 1.0),
        'p': nrm(ks[1], (DEPTH, BATCH, SEQ, PLE_DIM), 1.0),
        'a_norm': gain(ks[2], (N_A_LAYERS, D_MODEL)),
        'a_in_proj': nrm(ks[3], (N_A_LAYERS, D_MODEL, D_IN_PROJ), D_MODEL ** -0.5),
        'a_conv_w': nrm(ks[4], (N_A_LAYERS, D_CONV, CONV_DIM), D_CONV ** -0.5),
        'a_conv_b': nrm(ks[5], (N_A_LAYERS, CONV_DIM), 0.02),
        'a_dt_bias': dt0 + jnp.log(-jnp.expm1(-dt0)),
        'a_A_log': jnp.log(jax.random.uniform(ks[7], (N_A_LAYERS, SSM_HEADS), f32, 1.0, 16.0)),
        'a_D': 1.0 + 0.1 * jax.random.normal(ks[8], (N_A_LAYERS, SSM_HEADS), f32),
        'a_gnorm': gain(ks[9], (N_A_LAYERS, D_INNER)),
        'a_out_proj': nrm(ks[10], (N_A_LAYERS, D_INNER, D_MODEL), D_INNER ** -0.5),
        'kv_norm': gain(ks[11], (D_MODEL,)),
        'w_kvf': nrm(ks[12], (D_MODEL, 2 * ATT_DIM + FOX_HEADS), D_MODEL ** -0.5),
        'b_f': jax.random.uniform(ks[13], (FOX_HEADS,), f32, 1.0, 6.0),
        'b_norm': gain(ks[14], (N_B_LAYERS, D_MODEL)),
        'b_wq': nrm(ks[15], (N_B_LAYERS, D_MODEL, ATT_DIM), D_MODEL ** -0.5),
        'b_wo': nrm(ks[16], (N_B_LAYERS, ATT_DIM, D_MODEL), ATT_DIM ** -0.5),
        'c_norm': gain(ks[17], (DEPTH, D_MODEL)),
        'c_wq': nrm(ks[18], (DEPTH, D_MODEL, PEER_HEADS * PEER_QUERY_DIM), D_MODEL ** -0.5),
        'c_subkeys': nrm(ks[19], (DEPTH, PEER_HEADS, 2, PEER_N_KEYS, PEER_HALF), PEER_HALF ** -0.5),
        'c_u': nrm(ks[20], (DEPTH, PEER_EXPERTS, D_MODEL), D_MODEL ** -0.5),
        'c_v': nrm(ks[21], (DEPTH, PEER_EXPERTS, D_MODEL), PEER_HEADS ** -0.5),
        'e_norm': gain(ks[22], (DEPTH, D_MODEL)),
        'e_wg': nrm(ks[23], (DEPTH, D_MODEL, D_MODEL), D_MODEL ** -0.5),
        'e_bg': nrm(ks[24], (DEPTH, D_MODEL), 0.02),
        'e_wp': nrm(ks[25], (DEPTH, PLE_DIM, D_MODEL), PLE_DIM ** -0.5),
        'f_norm': gain(ks[26], (D_MODEL,)),
    }


def reference(x, p, a_norm, a_in_proj, a_conv_w, a_conv_b, a_dt_bias, a_A_log, a_D, a_gnorm, a_out_proj,
              kv_norm, w_kvf, b_f, b_norm, b_wq, b_wo,
              c_norm, c_wq, c_subkeys, c_u, c_v,
              e_norm, e_wg, e_bg, e_wp, f_norm):
    h = x
    shared = None
    for i in range(DEPTH):
        if i < N_A_LAYERS:
            h = h + mamba2_mixer(h, a_norm[i], a_in_proj[i], a_conv_w[i], a_conv_b[i], a_dt_bias[i],
                                 a_A_log[i], a_D[i], a_gnorm[i], a_out_proj[i])
        else:
            j = i - N_A_LAYERS
            k, v, cum = shared
            h = h + fox_attention(h, b_norm[j], b_wq[j], b_wo[j], k, v, cum)
        h = h + peer_ffn(h, c_norm[i], c_wq[i], c_subkeys[i], c_u[i], c_v[i])
        h = h + per_layer_embedding(h, p[i], e_norm[i], e_wg[i], e_bg[i], e_wp[i])
        if i == N_A_LAYERS - 1:
            shared = fox_shared_kv(h, kv_norm, w_kvf, b_f)
    return rms_norm(h, f_norm)
```

```python
import functools
import math

import jax
import jax.numpy as jnp
from jax import lax
from jax.experimental import pallas as pl
from jax.experimental.pallas import tpu as pltpu

F32 = jnp.float32
BF16 = jnp.bfloat16
NEG_INF = float("-inf")

D_MODEL = 2048
RMS_EPS = 1e-6
GATED_NORM_EPS = 1e-5

D_INNER = 4096
SSM_HEAD_DIM = 64
SSM_HEADS = 64
SSM_GROUPS = 8
SSM_HEADS_PER_GROUP = SSM_HEADS // SSM_GROUPS
SSM_STATE = 128
D_CONV = 4
CONV_DIM = D_INNER + 2 * SSM_GROUPS * SSM_STATE
SSD_CHUNK = 256
GROUP_CH = D_INNER // SSM_GROUPS

FOX_HEADS = 16
FOX_HEAD_DIM = 128
ATT_DIM = FOX_HEADS * FOX_HEAD_DIM
FOX_SCALE = FOX_HEAD_DIM ** -0.5

PEER_HEADS = 8
PEER_N_KEYS = 128
PEER_EXPERTS = PEER_N_KEYS * PEER_N_KEYS
PEER_TOPK = 16
PEER_HALF = 128
PLE_DIM = 256

LANES = 128
SUBLANES = 8
VMEM_LIMIT = 56 * 1024 * 1024


def _cparams(sem):
    return pltpu.CompilerParams(dimension_semantics=sem, vmem_limit_bytes=VMEM_LIMIT)


def _rms(x, w, eps):
    return x * lax.rsqrt(jnp.mean(x * x, axis=-1, keepdims=True) + eps) * w


def _rms_mm_kernel(x_ref, nw_ref, w_ref, *rest, out_scale, emit_xn):
    if emit_xn:
        o_ref, xn_out_ref, xn_ref = rest
    else:
        o_ref, xn_ref = rest

    @pl.when(pl.program_id(1) == 0)
    def _():
        xn = _rms(x_ref[...], nw_ref[...], RMS_EPS).astype(BF16)
        xn_ref[...] = xn
        if emit_xn:
            xn_out_ref[...] = xn

    acc = jnp.dot(xn_ref[...], w_ref[...], preferred_element_type=F32)
    if out_scale != 1.0:
        acc = acc * out_scale
    o_ref[...] = acc.astype(o_ref.dtype)


def rms_mm(x, nw, w, out_dtype, *, tm=512, tn=512, out_scale=1.0, emit_xn=False):
    s, d = x.shape
    n = w.shape[1]
    tn = min(tn, n)
    assert s % tm == 0 and n % tn == 0
    out_shape = [jax.ShapeDtypeStruct((s, n), out_dtype)]
    out_specs = [pl.BlockSpec((tm, tn), lambda i, j: (i, j))]
    if emit_xn:
        out_shape.append(jax.ShapeDtypeStruct((s, d), BF16))
        out_specs.append(pl.BlockSpec((tm, d), lambda i, j: (i, 0)))
    res = pl.pallas_call(
        functools.partial(_rms_mm_kernel, out_scale=out_scale, emit_xn=emit_xn),
        out_shape=out_shape,
        grid=(s // tm, n // tn),
        in_specs=[
            pl.BlockSpec((tm, d), lambda i, j: (i, 0)),
            pl.BlockSpec((1, d), lambda i, j: (0, 0)),
            pl.BlockSpec((d, tn), lambda i, j: (0, j)),
        ],
        out_specs=out_specs,
        scratch_shapes=[pltpu.VMEM((tm, d), BF16)],
        compiler_params=_cparams(("arbitrary", "arbitrary")),
        name="rms_mm",
    )(x, nw.reshape(1, d), w)
    return res if emit_xn else res[0]


def _mm_res_kernel(a_ref, w_ref, r_ref, o_ref):
    o_ref[...] = r_ref[...] + jnp.dot(a_ref[...], w_ref[...], preferred_element_type=F32)


def mm_res(a, w, res, *, tm=512, tn=512):
    s, k = a.shape
    n = w.shape[1]
    return pl.pallas_call(
        _mm_res_kernel,
        out_shape=jax.ShapeDtypeStruct((s, n), F32),
        grid=(s // tm, n // tn),
        in_specs=[
            pl.BlockSpec((tm, k), lambda i, j: (i, 0)),
            pl.BlockSpec((k, tn), lambda i, j: (0, j)),
            pl.BlockSpec((tm, tn), lambda i, j: (i, j)),
        ],
        out_specs=pl.BlockSpec((tm, tn), lambda i, j: (i, j)),
        compiler_params=_cparams(("arbitrary", "arbitrary")),
        name="mm_res",
    )(a, w, res)


def _ple_kernel(h_ref, y_ref, nw_ref, wg_ref, bg_ref, p_ref, wp_ref, o_ref, h1_ref, xn_ref, *, tn):
    j = pl.program_id(1)

    @pl.when(j == 0)
    def _():
        h1 = h_ref[...] + y_ref[...]
        h1_ref[...] = h1
        xn_ref[...] = _rms(h1, nw_ref[...], RMS_EPS).astype(BF16)

    gate = jax.nn.sigmoid(jnp.dot(xn_ref[...], wg_ref[...], preferred_element_type=F32) + bg_ref[...])
    proj = jnp.dot(p_ref[...], wp_ref[...], preferred_element_type=F32)
    col = pl.multiple_of(j * tn, tn)
    o_ref[...] = h1_ref[:, pl.ds(col, tn)] + gate * proj


def ple_layer(h, y, nw, wg, bg, p, wp, *, tm=512, tn=512):
    s, d = h.shape
    pd = p.shape[1]
    return pl.pallas_call(
        functools.partial(_ple_kernel, tn=tn),
        out_shape=jax.ShapeDtypeStruct((s, d), F32),
        grid=(s // tm, d // tn),
        in_specs=[
            pl.BlockSpec((tm, d), lambda i, j: (i, 0)),
            pl.BlockSpec((tm, d), lambda i, j: (i, 0)),
            pl.BlockSpec((1, d), lambda i, j: (0, 0)),
            pl.BlockSpec((d, tn), lambda i, j: (0, j)),
            pl.BlockSpec((1, tn), lambda i, j: (0, j)),
            pl.BlockSpec((tm, pd), lambda i, j: (i, 0)),
            pl.BlockSpec((pd, tn), lambda i, j: (0, j)),
        ],
        out_specs=pl.BlockSpec((tm, tn), lambda i, j: (i, j)),
        scratch_shapes=[pltpu.VMEM((tm, d), F32), pltpu.VMEM((tm, d), BF16)],
        compiler_params=_cparams(("arbitrary", "arbitrary")),
        name="ple",
    )(h, y, nw.reshape(1, d), wg, bg.reshape(1, d), p, wp)


def _final_norm_kernel(x_ref, w_ref, o_ref):
    o_ref[...] = _rms(x_ref[...], w_ref[...], RMS_EPS)


def final_norm(x, w, *, tm=512):
    s, d = x.shape
    return pl.pallas_call(
        _final_norm_kernel,
        out_shape=jax.ShapeDtypeStruct((s, d), F32),
        grid=(s // tm,),
        in_specs=[pl.BlockSpec((tm, d), lambda i: (i, 0)), pl.BlockSpec((1, d), lambda i: (0, 0))],
        out_specs=pl.BlockSpec((tm, d), lambda i: (i, 0)),
        compiler_params=_cparams(("arbitrary",)),
        name="final_norm",
    )(x, w.reshape(1, d))


def _tril_ones(n):
    r = lax.broadcasted_iota(jnp.int32, (n, n), 0)
    c = lax.broadcasted_iota(jnp.int32, (n, n), 1)
    return (c <= r).astype(F32)


def _dt_prep_kernel(raw_ref, bias_ref, alog_ref, dt_ref, ac_ref):
    x = raw_ref[...] + bias_ref[...]
    dt = jnp.maximum(x, 0.0) + jnp.log1p(jnp.exp(-jnp.abs(x)))
    a = -jnp.exp(alog_ref[...])
    dt_ref[...] = dt
    ac_ref[...] = jnp.dot(_tril_ones(SSD_CHUNK), dt * a, preferred_element_type=F32,
                          precision=lax.Precision.HIGHEST)


def dt_prep(dt_raw, dt_bias, a_log):
    s, w = dt_raw.shape
    row = pl.BlockSpec((SSD_CHUNK, w), lambda c: (c, 0))
    vec = pl.BlockSpec((1, w), lambda c: (0, 0))
    return pl.pallas_call(
        _dt_prep_kernel,
        out_shape=[jax.ShapeDtypeStruct((s, w), F32)] * 2,
        grid=(s // SSD_CHUNK,),
        in_specs=[row, vec, vec],
        out_specs=[row, row],
        compiler_params=_cparams(("arbitrary",)),
        name="dt_prep",
    )(dt_raw, dt_bias, a_log)


def _silu(x):
    return x * jax.nn.sigmoid(x)


def _ssd_kernel(z_ref, x_ref, b_ref, c_ref, wx_ref, wb_ref, wc_ref, bx_ref, bb_ref, bc_ref,
                dt_ref, ac_ref, act_ref, dexp_ref, gw_ref, expand_ref, o_ref,
                xbuf, bbuf, cbuf, state_ref, yint_ref):
    L = SSD_CHUNK
    P = SSM_HEAD_DIM
    R = SSM_HEADS_PER_GROUP
    H = D_CONV - 1
    c_idx = pl.program_id(1)

    @pl.when(c_idx == 0)
    def _():
        xbuf[0:SUBLANES, :] = jnp.zeros((SUBLANES, GROUP_CH), F32)
        bbuf[0:SUBLANES, :] = jnp.zeros((SUBLANES, SSM_STATE), F32)
        cbuf[0:SUBLANES, :] = jnp.zeros((SUBLANES, SSM_STATE), F32)
        state_ref[...] = jnp.zeros_like(state_ref)

    def conv(buf, cur_ref, w_ref, bias_ref):
        buf[SUBLANES:SUBLANES + L, :] = cur_ref[...]
        y = bias_ref[...] + w_ref[D_CONV - 1:D_CONV, :] * cur_ref[...]
        for k in range(H):
            y = y + w_ref[k:k + 1, :] * buf[pl.ds(SUBLANES - H + k, L), :]
        buf[0:SUBLANES, :] = cur_ref[L - SUBLANES:L, :]
        return _silu(y)

    xg = conv(xbuf, x_ref, wx_ref, bx_ref)
    bm = conv(bbuf, b_ref, wb_ref, bb_ref)
    cm = conv(cbuf, c_ref, wc_ref, bc_ref)
    bb = bm.astype(BF16)
    cb16 = cm.astype(BF16)

    dt8 = dt_ref[...]
    ac8 = ac_ref[...]
    act8 = act_ref[...]
    last8 = ac8[L - 1:L, :]

    expand = expand_ref[...]

    def per_channel(cols):
        return jnp.dot(cols, expand, preferred_element_type=F32, precision=lax.Precision.HIGHEST)

    dt_e = per_channel(dt8)
    eac_e = per_channel(jnp.exp(ac8))
    w_e = per_channel(jnp.exp(last8 - ac8) * dt8)
    elast_e = eac_e[L - 1:L, :]

    xdt = (xg * dt_e).astype(BF16)
    xw = (xg * w_e).astype(BF16)

    cbm = lax.dot_general(cb16, bb, (((1,), (1,)), ((), ())), preferred_element_type=F32)
    row = lax.broadcasted_iota(jnp.int32, (L, L), 0)
    col = lax.broadcasted_iota(jnp.int32, (L, L), 1)
    tril = col <= row
    for r in range(R):
        seg = ac8[:, r:r + 1] - act8[r:r + 1, :]
        decay = jnp.exp(jnp.where(tril, seg, NEG_INF))
        m = (cbm * decay).astype(BF16)
        yint_ref[:, r * P:(r + 1) * P] = jnp.dot(m, xdt[:, r * P:(r + 1) * P],
                                                  preferred_element_type=F32)

    state = state_ref[...]
    y_inter = jnp.dot(cb16, state.astype(BF16), preferred_element_type=F32) * eac_e
    bt = bm.T.astype(BF16)
    state_ref[...] = state * elast_e + jnp.dot(bt, xw, preferred_element_type=F32)

    y = yint_ref[...] + y_inter + dexp_ref[...] * xg
    gf = y * _silu(z_ref[...])
    o_ref[...] = _rms(gf, gw_ref[...], GATED_NORM_EPS).astype(o_ref.dtype)


def ssd_mixer(zx, conv_w, conv_b, dtg, acg, act, d_exp, gnorm_w):
    s = zx.shape[0]
    L = SSD_CHUNK
    G = SSM_GROUPS
    R = SSM_HEADS_PER_GROUP
    gn = SSM_STATE
    xoff = D_INNER // GROUP_CH
    boff = (2 * D_INNER) // gn
    coff = boff + G
    wboff = D_INNER // gn
    wcoff = wboff + G
    expand = (jnp.arange(GROUP_CH)[None, :] // SSM_HEAD_DIM == jnp.arange(R)[:, None]).astype(F32)
    in_specs = [
        pl.BlockSpec((L, GROUP_CH), lambda g, c: (c, g)),
        pl.BlockSpec((L, GROUP_CH), lambda g, c: (c, xoff + g)),
        pl.BlockSpec((L, gn), lambda g, c: (c, boff + g)),
        pl.BlockSpec((L, gn), lambda g, c: (c, coff + g)),
        pl.BlockSpec((D_CONV, GROUP_CH), lambda g, c: (0, g)),
        pl.BlockSpec((D_CONV, gn), lambda g, c: (0, wboff + g)),
        pl.BlockSpec((D_CONV, gn), lambda g, c: (0, wcoff + g)),
        pl.BlockSpec((1, GROUP_CH), lambda g, c: (0, g)),
        pl.BlockSpec((1, gn), lambda g, c: (0, wboff + g)),
        pl.BlockSpec((1, gn), lambda g, c: (0, wcoff + g)),
        pl.BlockSpec((None, L, R), lambda g, c: (g, c, 0)),
        pl.BlockSpec((None, L, R), lambda g, c: (g, c, 0)),
        pl.BlockSpec((None, R, L), lambda g, c: (g, 0, c)),
        pl.BlockSpec((1, GROUP_CH), lambda g, c: (0, g)),
        pl.BlockSpec((1, GROUP_CH), lambda g, c: (0, g)),
        pl.BlockSpec((R, GROUP_CH), lambda g, c: (0, 0)),
    ]
    return pl.pallas_call(
        _ssd_kernel,
        out_shape=jax.ShapeDtypeStruct((s, D_INNER), BF16),
        grid=(G, s // L),
        in_specs=in_specs,
        out_specs=pl.BlockSpec((L, GROUP_CH), lambda g, c: (c, g)),
        scratch_shapes=[
            pltpu.VMEM((L + SUBLANES, GROUP_CH), F32),
            pltpu.VMEM((L + SUBLANES, gn), F32),
            pltpu.VMEM((L + SUBLANES, gn), F32),
            pltpu.VMEM((gn, GROUP_CH), F32),
            pltpu.VMEM((L, GROUP_CH), F32),
        ],
        compiler_params=_cparams(("arbitrary", "arbitrary")),
        name="ssd",
    )(zx, zx, zx, zx, conv_w, conv_w, conv_w, conv_b, conv_b, conv_b,
      dtg, acg, act, d_exp, gnorm_w, expand)


def mamba_layer(h, norm_w, w_in, conv_w, conv_b, dt_bias, a_log, d_skip, gnorm_w, w_out):
    s = h.shape[0]
    n_zx = D_INNER + CONV_DIM
    w_zx = w_in[:, :n_zx].astype(BF16)
    w_dt = jnp.pad(w_in[:, n_zx:], ((0, 0), (0, LANES - SSM_HEADS))).astype(BF16)
    zx = rms_mm(h, norm_w, w_zx, F32)
    dt_raw = rms_mm(h, norm_w, w_dt, F32)
    pad = lambda v: jnp.pad(v.astype(F32), (0, LANES - SSM_HEADS)).reshape(1, LANES)
    dt, ac = dt_prep(dt_raw, pad(dt_bias), pad(a_log))
    grp = lambda t: t[:, :SSM_HEADS].reshape(s, SSM_GROUPS, SSM_HEADS_PER_GROUP).transpose(1, 0, 2)
    dtg, acg = grp(dt), grp(ac)
    act = acg.transpose(0, 2, 1)
    d_exp = jnp.repeat(d_skip.astype(F32), SSM_HEAD_DIM).reshape(1, D_INNER)
    y = ssd_mixer(zx, conv_w, conv_b.reshape(1, CONV_DIM), dtg, acg, act, d_exp,
                  gnorm_w.reshape(1, D_INNER))
    return mm_res(y, w_out.astype(BF16), h)


CUM_TILE = 256


def _cum_kernel(f_ref, b_ref, o_ref, carry_ref):
    @pl.when(pl.program_id(0) == 0)
    def _():
        carry_ref[...] = jnp.zeros_like(carry_ref)

    log_f = jax.nn.log_sigmoid(f_ref[...] + b_ref[...])
    cum = jnp.dot(_tril_ones(CUM_TILE), log_f, preferred_element_type=F32,
                  precision=lax.Precision.HIGHEST) + carry_ref[...]
    o_ref[...] = cum
    carry_ref[...] = cum[CUM_TILE - 1:CUM_TILE, :]


def fox_cum(f_raw, b_f):
    s, w = f_raw.shape
    return pl.pallas_call(
        _cum_kernel,
        out_shape=jax.ShapeDtypeStruct((s, w), F32),
        grid=(s // CUM_TILE,),
        in_specs=[pl.BlockSpec((CUM_TILE, w), lambda i: (i, 0)), pl.BlockSpec((1, w), lambda i: (0, 0))],
        out_specs=pl.BlockSpec((CUM_TILE, w), lambda i: (i, 0)),
        scratch_shapes=[pltpu.VMEM((1, w), F32)],
        compiler_params=_cparams(("arbitrary",)),
        name="fox_cum",
    )(f_raw, b_f)


FOX_TILE = 512


def _fox_kernel(q_ref, k_ref, v_ref, cq_ref, ck_ref, o_ref, m_ref, l_ref, acc_ref):
    i = pl.program_id(1)
    j = pl.program_id(2)
    T = FOX_TILE

    @pl.when(j == 0)
    def _():
        m_ref[...] = jnp.full_like(m_ref, NEG_INF)
        l_ref[...] = jnp.zeros_like(l_ref)
        acc_ref[...] = jnp.zeros_like(acc_ref)

    def step(masked):
        s = lax.dot_general(q_ref[...], k_ref[...], (((1,), (1,)), ((), ())),
                            preferred_element_type=F32)
        z = s - ck_ref[...]
        if masked:
            row = lax.broadcasted_iota(jnp.int32, (T, T), 0)
            col = lax.broadcasted_iota(jnp.int32, (T, T), 1)
            z = jnp.where(col <= row, z, NEG_INF)
        cq = cq_ref[...]
        m_old = m_ref[...]
        m_new = jnp.maximum(m_old, jnp.max(z, axis=-1, keepdims=True) + cq)
        p = jnp.exp(z + (cq - m_new))
        alpha = jnp.exp(m_old - m_new)
        l_ref[...] = alpha * l_ref[...] + jnp.sum(p, axis=-1, keepdims=True)
        acc_ref[...] = alpha * acc_ref[...] + jnp.dot(p.astype(BF16), v_ref[...],
                                                      preferred_element_type=F32)
        m_ref[...] = m_new

    @pl.when(j < i)
    def _():
        step(False)

    @pl.when(j == i)
    def _():
        step(True)
        o_ref[...] = (acc_ref[...] / l_ref[...]).astype(o_ref.dtype)


def fox_attention(q, kv, cum):
    s = q.shape[0]
    T = FOX_TILE
    nt = s // T
    cq = cum[:, :FOX_HEADS].T.reshape(FOX_HEADS, s, 1)
    ck = cum[:, :FOX_HEADS].T.reshape(FOX_HEADS, 1, s)
    kvj = lambda i, j: jnp.minimum(i, j)
    return pl.pallas_call(
        _fox_kernel,
        out_shape=jax.ShapeDtypeStruct((s, ATT_DIM), BF16),
        grid=(FOX_HEADS, nt, nt),
        in_specs=[
            pl.BlockSpec((T, FOX_HEAD_DIM), lambda h, i, j: (i, h)),
            pl.BlockSpec((T, FOX_HEAD_DIM), lambda h, i, j: (kvj(i, j), h)),
            pl.BlockSpec((T, FOX_HEAD_DIM), lambda h, i, j: (kvj(i, j), FOX_HEADS + h)),
            pl.BlockSpec((None, T, 1), lambda h, i, j: (h, i, 0)),
            pl.BlockSpec((None, 1, T), lambda h, i, j: (h, 0, kvj(i, j))),
        ],
        out_specs=pl.BlockSpec((T, FOX_HEAD_DIM), lambda h, i, j: (i, h)),
        scratch_shapes=[pltpu.VMEM((T, 1), F32), pltpu.VMEM((T, 1), F32),
                        pltpu.VMEM((T, FOX_HEAD_DIM), F32)],
        compiler_params=_cparams(("arbitrary", "arbitrary", "arbitrary")),
        name="fox_attn",
    )(q, kv, kv, cq, ck)


def fox_shared_kv(h, kv_norm, w_kvf, b_f):
    w_kv = w_kvf[:, :2 * ATT_DIM].astype(BF16)
    w_f = jnp.pad(w_kvf[:, 2 * ATT_DIM:], ((0, 0), (0, LANES - FOX_HEADS))).astype(BF16)
    kv = rms_mm(h, kv_norm, w_kv, BF16)
    f_raw = rms_mm(h, kv_norm, w_f, F32)
    b_pad = jnp.pad(b_f.astype(F32), (0, LANES - FOX_HEADS)).reshape(1, LANES)
    return kv, fox_cum(f_raw, b_pad)


def fox_layer(h, norm_w, w_q, w_o, kv, cum):
    q = rms_mm(h, norm_w, w_q.astype(BF16), BF16, out_scale=FOX_SCALE)
    o = fox_attention(q, kv, cum)
    return mm_res(o, w_o.astype(BF16), h)


ROUTE_TILE = 256
N_TOP = PEER_TOPK + 1
TOP_ROWS = 24
_CAND_NB = [N_TOP // (a + 1) for a in range(1, SUBLANES)]
CAND_ROWS = TOP_ROWS + SUBLANES * (SUBLANES - 1) + (TOP_ROWS - SUBLANES)


def _extract_top(x, out_ref, n):
    for k in range(n):
        m = jnp.max(x, axis=0, keepdims=True)
        out_ref[k:k + 1, :] = m
        x = jnp.where(x >= m, NEG_INF, x)


def _route_kernel(q_ref, keys_ref, s1p_ref, s2_ref, taup_ref, v1_ref, v2_ref, ct_ref):
    T = ROUTE_TILE

    def head(h, carry):
        v1_ref[...] = jnp.full_like(v1_ref, NEG_INF)
        v2_ref[...] = jnp.full_like(v2_ref, NEG_INF)
        ct_ref[...] = jnp.full_like(ct_ref, NEG_INF)
        col = pl.multiple_of(h * 2 * PEER_HALF, 2 * PEER_HALF)
        q1 = q_ref[:, pl.ds(col, PEER_HALF)]
        q2 = q_ref[:, pl.ds(col + PEER_HALF, PEER_HALF)]
        nt = (((1,), (1,)), ((), ()))
        s1 = lax.dot_general(keys_ref[h, 0], q1, nt, preferred_element_type=F32)
        s2 = lax.dot_general(keys_ref[h, 1], q2, nt, preferred_element_type=F32)
        _extract_top(s1, v1_ref, N_TOP)
        _extract_top(s2, v2_ref, N_TOP)
        v1 = v1_ref[...]
        v2 = v2_ref[...]
        pieces = [v1[0:1, :] + v2]
        rows8 = lax.broadcasted_iota(jnp.int32, (SUBLANES, T), 0)
        for a in range(1, SUBLANES):
            pieces.append(v1[a:a + 1, :] + jnp.where(rows8 < _CAND_NB[a - 1], v2[0:SUBLANES, :], NEG_INF))
        pieces.append(v1[SUBLANES:TOP_ROWS, :] + v2[0:1, :])
        cand = jnp.concatenate(pieces, axis=0)
        _extract_top(cand, ct_ref, N_TOP)
        m = ct_ref[0:1, :]
        tau = 0.5 * (ct_ref[PEER_TOPK - 1:PEER_TOPK, :] + ct_ref[PEER_TOPK:PEER_TOPK + 1, :])
        zsum = jnp.sum(jnp.where(cand >= tau, jnp.exp(cand - m), 0.0), axis=0, keepdims=True)
        shift = m + jnp.log(zsum)
        s1p_ref[h] = s1 - shift
        s2_ref[h] = s2
        taup_ref[pl.ds(h, 1), :] = tau - shift
        return carry

    lax.fori_loop(0, PEER_HEADS, head, 0)


def peer_route(q, keys):
    s = q.shape[0]
    T = ROUTE_TILE
    big = pl.BlockSpec((PEER_HEADS, PEER_N_KEYS, T), lambda t: (0, 0, t))
    return pl.pallas_call(
        _route_kernel,
        out_shape=[jax.ShapeDtypeStruct((PEER_HEADS, PEER_N_KEYS, s), F32)] * 2
        + [jax.ShapeDtypeStruct((PEER_HEADS, s), F32)],
        grid=(s // T,),
        in_specs=[
            pl.BlockSpec((T, PEER_HEADS * 2 * PEER_HALF), lambda t: (t, 0)),
            pl.BlockSpec((PEER_HEADS, 2, PEER_N_KEYS, PEER_HALF), lambda t: (0, 0, 0, 0)),
        ],
        out_specs=[big, big, pl.BlockSpec((PEER_HEADS, T), lambda t: (0, t))],
        scratch_shapes=[pltpu.VMEM((TOP_ROWS, T), F32), pltpu.VMEM((TOP_ROWS, T), F32),
                        pltpu.VMEM((TOP_ROWS, T), F32)],
        compiler_params=_cparams(("arbitrary",)),
        name="peer_route",
    )(q, keys)


PEER_TOK_TILE = 512
PEER_EXP_TILE = 512
_SQRT_HALF = 0.7071067811865476


def _gelu(x):
    return 0.5 * x * (1.0 + lax.erf(x * _SQRT_HALF))


def _peer_kernel(xn_ref, u_ref, vt_ref, s1p_ref, s2_ref, taup_ref, o_ref, at_ref, ga_ref, acc_ref):
    e = pl.program_id(1)
    T = PEER_TOK_TILE
    n_i = PEER_EXP_TILE // PEER_N_KEYS

    @pl.when(e == 0)
    def _():
        acc_ref[...] = jnp.zeros_like(acc_ref)

    at_ref[...] = lax.dot_general(u_ref[...], xn_ref[...], (((1,), (1,)), ((), ())),
                                  preferred_element_type=F32)

    def chunk(tc, carry):
        lane0 = pl.multiple_of(tc * LANES, LANES)
        for ii in range(n_i):
            rows = slice(ii * PEER_N_KEYS, (ii + 1) * PEER_N_KEYS)
            g = jnp.zeros((PEER_N_KEYS, LANES), F32)
            for h in range(PEER_HEADS):
                ssum = s1p_ref[h, ii:ii + 1, pl.ds(lane0, LANES)] + s2_ref[h, :, pl.ds(lane0, LANES)]
                g = g + jnp.where(ssum >= taup_ref[h:h + 1, pl.ds(lane0, LANES)], jnp.exp(ssum), 0.0)
            a = at_ref[rows, pl.ds(lane0, LANES)]
            ga_ref[rows, pl.ds(lane0, LANES)] = (g * _gelu(a)).astype(BF16)
        return carry

    lax.fori_loop(0, T // LANES, chunk, 0)

    acc_ref[...] += jnp.dot(vt_ref[...], ga_ref[...], preferred_element_type=F32)

    @pl.when(e == pl.num_programs(1) - 1)
    def _():
        o_ref[...] = acc_ref[...].T


def peer_experts(xn, u, vt, s1p, s2, taup):
    s, d = xn.shape
    T = PEER_TOK_TILE
    Et = PEER_EXP_TILE
    n_i = Et // PEER_N_KEYS
    s1p = s1p.reshape(PEER_HEADS, PEER_N_KEYS // n_i, n_i, s)
    route1 = pl.BlockSpec((PEER_HEADS, None, n_i, T), lambda t, e: (0, e, 0, t))
    route = pl.BlockSpec((PEER_HEADS, PEER_N_KEYS, T), lambda t, e: (0, 0, t))
    return pl.pallas_call(
        _peer_kernel,
        out_shape=jax.ShapeDtypeStruct((s, d), F32),
        grid=(s // T, PEER_EXPERTS // Et),
        in_specs=[
            pl.BlockSpec((T, d), lambda t, e: (t, 0)),
            pl.BlockSpec((Et, d), lambda t, e: (e, 0)),
            pl.BlockSpec((d, Et), lambda t, e: (0, e)),
            route1, route,
            pl.BlockSpec((PEER_HEADS, T), lambda t, e: (0, t)),
        ],
        out_specs=pl.BlockSpec((T, d), lambda t, e: (t, 0)),
        scratch_shapes=[pltpu.VMEM((Et, T), F32), pltpu.VMEM((Et, T), BF16), pltpu.VMEM((d, T), F32)],
        compiler_params=_cparams(("arbitrary", "arbitrary")),
        name="peer_experts",
    )(xn, u, vt, s1p, s2, taup)


def peer_layer(h, norm_w, w_q, sub_keys, u_tab, v_tab):
    q, xn = rms_mm(h, norm_w, w_q.astype(BF16), BF16, emit_xn=True)
    s1p, s2, taup = peer_route(q, sub_keys.astype(BF16))
    return peer_experts(xn, u_tab.astype(BF16), v_tab.T.astype(BF16), s1p, s2, taup)


def _trunk(x, p, a_norm, a_in_proj, a_conv_w, a_conv_b, a_dt_bias, a_A_log, a_D, a_gnorm, a_out_proj,
           kv_norm, w_kvf, b_f, b_norm, b_wq, b_wo, c_norm, c_wq, c_subkeys, c_u, c_v,
           e_norm, e_wg, e_bg, e_wp, f_norm):
    depth = p.shape[0]
    n_a = a_norm.shape[0]
    h = x[0]
    shared = None
    for i in range(depth):
        if i < n_a:
            h = mamba_layer(h, a_norm[i], a_in_proj[i], a_conv_w[i], a_conv_b[i], a_dt_bias[i],
                            a_A_log[i], a_D[i], a_gnorm[i], a_out_proj[i])
        else:
            j = i - n_a
            h = fox_layer(h, b_norm[j], b_wq[j], b_wo[j], *shared)
        y = peer_layer(h, c_norm[i], c_wq[i], c_subkeys[i], c_u[i], c_v[i])
        h = ple_layer(h, y, e_norm[i], e_wg[i].astype(BF16), e_bg[i], p[i, 0].astype(BF16),
                      e_wp[i].astype(BF16))
        if i == n_a - 1:
            shared = fox_shared_kv(h, kv_norm, w_kvf, b_f)
    return final_norm(h, f_norm)[None]


def kernel(x, p, a_norm, a_in_proj, a_conv_w, a_conv_b, a_dt_bias, a_A_log, a_D, a_gnorm, a_out_proj, kv_norm, w_kvf, b_f, b_norm, b_wq, b_wo, c_norm, c_wq, c_subkeys, c_u, c_v, e_norm, e_wg, e_bg, e_wp, f_norm):
    assert x.shape[0] == 1
    return _trunk(x, p, a_norm, a_in_proj, a_conv_w, a_conv_b, a_dt_bias, a_A_log, a_D, a_gnorm,
                  a_out_proj, kv_norm, w_kvf, b_f, b_norm, b_wq, b_wo, c_norm, c_wq, c_subkeys,
                  c_u, c_v, e_norm, e_wg, e_bg, e_wp, f_norm)
```

```python
import functools
import math

import jax
import jax.numpy as jnp
from jax import lax
from jax.experimental import pallas as pl
from jax.experimental.pallas import tpu as pltpu

F32 = jnp.float32
BF16 = jnp.bfloat16
NEG_INF = float("-inf")

D_MODEL = 2048
RMS_EPS = 1e-6
GATED_NORM_EPS = 1e-5

D_INNER = 4096
SSM_HEAD_DIM = 64
SSM_HEADS = 64
SSM_GROUPS = 8
SSM_HEADS_PER_GROUP = SSM_HEADS // SSM_GROUPS
SSM_STATE = 128
D_CONV = 4
CONV_DIM = D_INNER + 2 * SSM_GROUPS * SSM_STATE
SSD_CHUNK = 256
GROUP_CH = D_INNER // SSM_GROUPS

FOX_HEADS = 16
FOX_HEAD_DIM = 128
ATT_DIM = FOX_HEADS * FOX_HEAD_DIM
FOX_SCALE = FOX_HEAD_DIM ** -0.5

PEER_HEADS = 8
PEER_N_KEYS = 128
PEER_EXPERTS = PEER_N_KEYS * PEER_N_KEYS
PEER_TOPK = 16
PEER_HALF = 128
PLE_DIM = 256

LANES = 128
SUBLANES = 8
VMEM_LIMIT = 56 * 1024 * 1024


def _cparams(sem):
    return pltpu.CompilerParams(dimension_semantics=sem, vmem_limit_bytes=VMEM_LIMIT)


def _rms(x, w, eps):
    return x * lax.rsqrt(jnp.mean(x * x, axis=-1, keepdims=True) + eps) * w


def _rms_mm_kernel(x_ref, nw_ref, w_ref, *rest, out_scale, emit_xn):
    if emit_xn:
        o_ref, xn_out_ref, xn_ref = rest
    else:
        o_ref, xn_ref = rest

    @pl.when(pl.program_id(1) == 0)
    def _():
        xn = _rms(x_ref[...], nw_ref[...], RMS_EPS).astype(BF16)
        xn_ref[...] = xn
        if emit_xn:
            xn_out_ref[...] = xn

    acc = jnp.dot(xn_ref[...], w_ref[...], preferred_element_type=F32)
    if out_scale != 1.0:
        acc = acc * out_scale
    o_ref[...] = acc.astype(o_ref.dtype)


def rms_mm(x, nw, w, out_dtype, *, tm=512, tn=512, out_scale=1.0, emit_xn=False):
    s, d = x.shape
    n = w.shape[1]
    tn = min(tn, n)
    assert s % tm == 0 and n % tn == 0
    out_shape = [jax.ShapeDtypeStruct((s, n), out_dtype)]
    out_specs = [pl.BlockSpec((tm, tn), lambda i, j: (i, j))]
    if emit_xn:
        out_shape.append(jax.ShapeDtypeStruct((s, d), BF16))
        out_specs.append(pl.BlockSpec((tm, d), lambda i, j: (i, 0)))
    res = pl.pallas_call(
        functools.partial(_rms_mm_kernel, out_scale=out_scale, emit_xn=emit_xn),
        out_shape=out_shape,
        grid=(s // tm, n // tn),
        in_specs=[
            pl.BlockSpec((tm, d), lambda i, j: (i, 0)),
            pl.BlockSpec((1, d), lambda i, j: (0, 0)),
            pl.BlockSpec((d, tn), lambda i, j: (0, j)),
        ],
        out_specs=out_specs,
        scratch_shapes=[pltpu.VMEM((tm, d), BF16)],
        compiler_params=_cparams(("arbitrary", "arbitrary")),
        name="rms_mm",
    )(x, nw.reshape(1, d), w)
    return res if emit_xn else res[0]


def _mm_res_kernel(a_ref, w_ref, r_ref, o_ref):
    o_ref[...] = r_ref[...] + jnp.dot(a_ref[...], w_ref[...], preferred_element_type=F32)


def mm_res(a, w, res, *, tm=512, tn=512):
    s, k = a.shape
    n = w.shape[1]
    return pl.pallas_call(
        _mm_res_kernel,
        out_shape=jax.ShapeDtypeStruct((s, n), F32),
        grid=(s // tm, n // tn),
        in_specs=[
            pl.BlockSpec((tm, k), lambda i, j: (i, 0)),
            pl.BlockSpec((k, tn), lambda i, j: (0, j)),
            pl.BlockSpec((tm, tn), lambda i, j: (i, j)),
        ],
        out_specs=pl.BlockSpec((tm, tn), lambda i, j: (i, j)),
        compiler_params=_cparams(("arbitrary", "arbitrary")),
        name="mm_res",
    )(a, w, res)


def _ple_kernel(h_ref, y_ref, nw_ref, wg_ref, bg_ref, p_ref, wp_ref, o_ref, h1_ref, xn_ref, *, tn):
    j = pl.program_id(1)

    @pl.when(j == 0)
    def _():
        h1 = h_ref[...] + y_ref[...]
        h1_ref[...] = h1
        xn_ref[...] = _rms(h1, nw_ref[...], RMS_EPS).astype(BF16)

    gate = jax.nn.sigmoid(jnp.dot(xn_ref[...], wg_ref[...], preferred_element_type=F32) + bg_ref[...])
    proj = jnp.dot(p_ref[...], wp_ref[...], preferred_element_type=F32)
    col = pl.multiple_of(j * tn, tn)
    o_ref[...] = h1_ref[:, pl.ds(col, tn)] + gate * proj


def ple_layer(h, y, nw, wg, bg, p, wp, *, tm=512, tn=512):
    s, d = h.shape
    pd = p.shape[1]
    return pl.pallas_call(
        functools.partial(_ple_kernel, tn=tn),
        out_shape=jax.ShapeDtypeStruct((s, d), F32),
        grid=(s // tm, d // tn),
        in_specs=[
            pl.BlockSpec((tm, d), lambda i, j: (i, 0)),
            pl.BlockSpec((tm, d), lambda i, j: (i, 0)),
            pl.BlockSpec((1, d), lambda i, j: (0, 0)),
            pl.BlockSpec((d, tn), lambda i, j: (0, j)),
            pl.BlockSpec((1, tn), lambda i, j: (0, j)),
            pl.BlockSpec((tm, pd), lambda i, j: (i, 0)),
            pl.BlockSpec((pd, tn), lambda i, j: (0, j)),
        ],
        out_specs=pl.BlockSpec((tm, tn), lambda i, j: (i, j)),
        scratch_shapes=[pltpu.VMEM((tm, d), F32), pltpu.VMEM((tm, d), BF16)],
        compiler_params=_cparams(("arbitrary", "arbitrary")),
        name="ple",
    )(h, y, nw.reshape(1, d), wg, bg.reshape(1, d), p, wp)


def _final_norm_kernel(x_ref, w_ref, o_ref):
    o_ref[...] = _rms(x_ref[...], w_ref[...], RMS_EPS)


def final_norm(x, w, *, tm=512):
    s, d = x.shape
    return pl.pallas_call(
        _final_norm_kernel,
        out_shape=jax.ShapeDtypeStruct((s, d), F32),
        grid=(s // tm,),
        in_specs=[pl.BlockSpec((tm, d), lambda i: (i, 0)), pl.BlockSpec((1, d), lambda i: (0, 0))],
        out_specs=pl.BlockSpec((tm, d), lambda i: (i, 0)),
        compiler_params=_cparams(("arbitrary",)),
        name="final_norm",
    )(x, w.reshape(1, d))


def _tril_ones(n):
    r = lax.broadcasted_iota(jnp.int32, (n, n), 0)
    c = lax.broadcasted_iota(jnp.int32, (n, n), 1)
    return (c <= r).astype(F32)


def _dt_prep_kernel(raw_ref, bias_ref, alog_ref, dt_ref, ac_ref):
    x = raw_ref[...] + bias_ref[...]
    dt = jnp.maximum(x, 0.0) + jnp.log1p(jnp.exp(-jnp.abs(x)))
    a = -jnp.exp(alog_ref[...])
    dt_ref[...] = dt
    ac_ref[...] = jnp.dot(_tril_ones(SSD_CHUNK), dt * a, preferred_element_type=F32,
                          precision=lax.Precision.HIGHEST)


def dt_prep(dt_raw, dt_bias, a_log):
    s, w = dt_raw.shape
    row = pl.BlockSpec((SSD_CHUNK, w), lambda c: (c, 0))
    vec = pl.BlockSpec((1, w), lambda c: (0, 0))
    return pl.pallas_call(
        _dt_prep_kernel,
        out_shape=[jax.ShapeDtypeStruct((s, w), F32)] * 2,
        grid=(s // SSD_CHUNK,),
        in_specs=[row, vec, vec],
        out_specs=[row, row],
        compiler_params=_cparams(("arbitrary",)),
        name="dt_prep",
    )(dt_raw, dt_bias, a_log)


def _silu(x):
    return x * jax.nn.sigmoid(x)


def _ssd_kernel(z_ref, x_ref, b_ref, c_ref, wx_ref, wb_ref, wc_ref, bx_ref, bb_ref, bc_ref,
                dt_ref, ac_ref, act_ref, dexp_ref, gw_ref, expand_ref, o_ref,
                xbuf, bbuf, cbuf, state_ref, yint_ref):
    L = SSD_CHUNK
    P = SSM_HEAD_DIM
    R = SSM_HEADS_PER_GROUP
    H = D_CONV - 1
    c_idx = pl.program_id(1)

    @pl.when(c_idx == 0)
    def _():
        xbuf[0:SUBLANES, :] = jnp.zeros((SUBLANES, GROUP_CH), F32)
        bbuf[0:SUBLANES, :] = jnp.zeros((SUBLANES, SSM_STATE), F32)
        cbuf[0:SUBLANES, :] = jnp.zeros((SUBLANES, SSM_STATE), F32)
        state_ref[...] = jnp.zeros_like(state_ref)

    def conv(buf, cur_ref, w_ref, bias_ref):
        buf[SUBLANES:SUBLANES + L, :] = cur_ref[...]
        y = bias_ref[...] + w_ref[D_CONV - 1:D_CONV, :] * cur_ref[...]
        for k in range(H):
            y = y + w_ref[k:k + 1, :] * buf[pl.ds(SUBLANES - H + k, L), :]
        buf[0:SUBLANES, :] = cur_ref[L - SUBLANES:L, :]
        return _silu(y)

    xg = conv(xbuf, x_ref, wx_ref, bx_ref)
    bm = conv(bbuf, b_ref, wb_ref, bb_ref)
    cm = conv(cbuf, c_ref, wc_ref, bc_ref)
    bb = bm.astype(BF16)
    cb16 = cm.astype(BF16)

    dt8 = dt_ref[...]
    ac8 = ac_ref[...]
    act8 = act_ref[...]
    last8 = ac8[L - 1:L, :]

    expand = expand_ref[...]

    def per_channel(cols):
        return jnp.dot(cols, expand, preferred_element_type=F32, precision=lax.Precision.HIGHEST)

    dt_e = per_channel(dt8)
    eac_e = per_channel(jnp.exp(ac8))
    w_e = per_channel(jnp.exp(last8 - ac8) * dt8)
    elast_e = eac_e[L - 1:L, :]

    xdt = (xg * dt_e).astype(BF16)
    xw = (xg * w_e).astype(BF16)

    cbm = lax.dot_general(cb16, bb, (((1,), (1,)), ((), ())), preferred_element_type=F32)
    row = lax.broadcasted_iota(jnp.int32, (L, L), 0)
    col = lax.broadcasted_iota(jnp.int32, (L, L), 1)
    tril = col <= row
    for r in range(R):
        seg = ac8[:, r:r + 1] - act8[r:r + 1, :]
        decay = jnp.exp(jnp.where(tril, seg, NEG_INF))
        m = (cbm * decay).astype(BF16)
        yint_ref[:, r * P:(r + 1) * P] = jnp.dot(m, xdt[:, r * P:(r + 1) * P],
                                                  preferred_element_type=F32)

    state = state_ref[...]
    y_inter = jnp.dot(cb16, state.astype(BF16), preferred_element_type=F32) * eac_e
    bt = bm.T.astype(BF16)
    state_ref[...] = state * elast_e + jnp.dot(bt, xw, preferred_element_type=F32)

    y = yint_ref[...] + y_inter + dexp_ref[...] * xg
    gf = y * _silu(z_ref[...])
    o_ref[...] = _rms(gf, gw_ref[...], GATED_NORM_EPS).astype(o_ref.dtype)


def ssd_mixer(zx, conv_w, conv_b, dtg, acg, act, d_exp, gnorm_w):
    s = zx.shape[0]
    L = SSD_CHUNK
    G = SSM_GROUPS
    R = SSM_HEADS_PER_GROUP
    gn = SSM_STATE
    xoff = D_INNER // GROUP_CH
    boff = (2 * D_INNER) // gn
    coff = boff + G
    wboff = D_INNER // gn
    wcoff = wboff + G
    expand = (jnp.arange(GROUP_CH)[None, :] // SSM_HEAD_DIM == jnp.arange(R)[:, None]).astype(F32)
    in_specs = [
        pl.BlockSpec((L, GROUP_CH), lambda g, c: (c, g)),
        pl.BlockSpec((L, GROUP_CH), lambda g, c: (c, xoff + g)),
        pl.BlockSpec((L, gn), lambda g, c: (c, boff + g)),
        pl.BlockSpec((L, gn), lambda g, c: (c, coff + g)),
        pl.BlockSpec((D_CONV, GROUP_CH), lambda g, c: (0, g)),
        pl.BlockSpec((D_CONV, gn), lambda g, c: (0, wboff + g)),
        pl.BlockSpec((D_CONV, gn), lambda g, c: (0, wcoff + g)),
        pl.BlockSpec((1, GROUP_CH), lambda g, c: (0, g)),
        pl.BlockSpec((1, gn), lambda g, c: (0, wboff + g)),
        pl.BlockSpec((1, gn), lambda g, c: (0, wcoff + g)),
        pl.BlockSpec((None, L, R), lambda g, c: (g, c, 0)),
        pl.BlockSpec((None, L, R), lambda g, c: (g, c, 0)),
        pl.BlockSpec((None, R, L), lambda g, c: (g, 0, c)),
        pl.BlockSpec((1, GROUP_CH), lambda g, c: (0, g)),
        pl.BlockSpec((1, GROUP_CH), lambda g, c: (0, g)),
        pl.BlockSpec((R, GROUP_CH), lambda g, c: (0, 0)),
    ]
    return pl.pallas_call(
        _ssd_kernel,
        out_shape=jax.ShapeDtypeStruct((s, D_INNER), BF16),
        grid=(G, s // L),
        in_specs=in_specs,
        out_specs=pl.BlockSpec((L, GROUP_CH), lambda g, c: (c, g)),
        scratch_shapes=[
            pltpu.VMEM((L + SUBLANES, GROUP_CH), F32),
            pltpu.VMEM((L + SUBLANES, gn), F32),
            pltpu.VMEM((L + SUBLANES, gn), F32),
            pltpu.VMEM((gn, GROUP_CH), F32),
            pltpu.VMEM((L, GROUP_CH), F32),
        ],
        compiler_params=_cparams(("arbitrary", "arbitrary")),
        name="ssd",
    )(zx, zx, zx, zx, conv_w, conv_w, conv_w, conv_b, conv_b, conv_b,
      dtg, acg, act, d_exp, gnorm_w, expand)


def mamba_layer(h, norm_w, w_in, conv_w, conv_b, dt_bias, a_log, d_skip, gnorm_w, w_out):
    s = h.shape[0]
    n_zx = D_INNER + CONV_DIM
    w_zx = w_in[:, :n_zx].astype(BF16)
    w_dt = jnp.pad(w_in[:, n_zx:], ((0, 0), (0, LANES - SSM_HEADS))).astype(BF16)
    zx = rms_mm(h, norm_w, w_zx, F32)
    dt_raw = rms_mm(h, norm_w, w_dt, F32)
    pad = lambda v: jnp.pad(v.astype(F32), (0, LANES - SSM_HEADS)).reshape(1, LANES)
    dt, ac = dt_prep(dt_raw, pad(dt_bias), pad(a_log))
    grp = lambda t: t[:, :SSM_HEADS].reshape(s, SSM_GROUPS, SSM_HEADS_PER_GROUP).transpose(1, 0, 2)
    dtg, acg = grp(dt), grp(ac)
    act = acg.transpose(0, 2, 1)
    d_exp = jnp.repeat(d_skip.astype(F32), SSM_HEAD_DIM).reshape(1, D_INNER)
    y = ssd_mixer(zx, conv_w, conv_b.reshape(1, CONV_DIM), dtg, acg, act, d_exp,
                  gnorm_w.reshape(1, D_INNER))
    return mm_res(y, w_out.astype(BF16), h)


CUM_TILE = 256


def _cum_kernel(f_ref, b_ref, o_ref, carry_ref):
    @pl.when(pl.program_id(0) == 0)
    def _():
        carry_ref[...] = jnp.zeros_like(carry_ref)

    log_f = jax.nn.log_sigmoid(f_ref[...] + b_ref[...])
    cum = jnp.dot(_tril_ones(CUM_TILE), log_f, preferred_element_type=F32,
                  precision=lax.Precision.HIGHEST) + carry_ref[...]
    o_ref[...] = cum
    carry_ref[...] = cum[CUM_TILE - 1:CUM_TILE, :]


def fox_cum(f_raw, b_f):
    s, w = f_raw.shape
    return pl.pallas_call(
        _cum_kernel,
        out_shape=jax.ShapeDtypeStruct((s, w), F32),
        grid=(s // CUM_TILE,),
        in_specs=[pl.BlockSpec((CUM_TILE, w), lambda i: (i, 0)), pl.BlockSpec((1, w), lambda i: (0, 0))],
        out_specs=pl.BlockSpec((CUM_TILE, w), lambda i: (i, 0)),
        scratch_shapes=[pltpu.VMEM((1, w), F32)],
        compiler_params=_cparams(("arbitrary",)),
        name="fox_cum",
    )(f_raw, b_f)


FOX_TILE = 512


FOX_AUG = 2 * FOX_HEAD_DIM
LOG2E = 1.4426950408889634


def _fox_kernel(q_ref, k_ref, v_ref, cq_ref, ck_ref, o_ref, kaug, vt, qaug, za, zb, m_ref, l_ref, acc_ref):
    i = pl.program_id(1)
    T = FOX_TILE
    D = FOX_HEAD_DIM
    lane = lax.broadcasted_iota(jnp.int32, (1, D), 1)

    @pl.when(i == 0)
    def _():
        c2 = ck_ref[...] * (-LOG2E)
        hi = c2.astype(BF16)
        r1 = c2 - hi.astype(F32)
        mid = r1.astype(BF16)
        lo = (r1 - mid.astype(F32)).astype(BF16)
        aug = jnp.where(lane == 0, hi.astype(F32),
                        jnp.where(lane == 1, mid.astype(F32),
                                  jnp.where(lane == 2, lo.astype(F32), 0.0)))
        kaug[:, 0:D] = k_ref[...]
        kaug[:, D:FOX_AUG] = aug.astype(BF16)
        vt[...] = v_ref[...].astype(F32).T.astype(BF16)

    qaug[:, 0:D] = q_ref[...]
    qaug[:, D:FOX_AUG] = jnp.broadcast_to(jnp.where(lane < 3, 1.0, 0.0), (T, D)).astype(BF16)
    m_ref[...] = jnp.full_like(m_ref, NEG_INF)
    l_ref[...] = jnp.zeros_like(l_ref)
    acc_ref[...] = jnp.zeros_like(acc_ref)
    cq2 = cq_ref[...] * LOG2E
    nt = (((1,), (1,)), ((), ()))

    def logits(j, z_ref):
        off = pl.multiple_of(j * T, T)
        z_ref[...] = lax.dot_general(kaug[pl.ds(off, T), :], qaug[...], nt,
                                     preferred_element_type=F32)

    def softmax_pv(j, z_ref, masked):
        off = pl.multiple_of(j * T, T)
        z = z_ref[...]
        if masked:
            key = lax.broadcasted_iota(jnp.int32, (T, T), 0)
            qry = lax.broadcasted_iota(jnp.int32, (T, T), 1)
            z = jnp.where(key <= qry, z, NEG_INF)
        m_old = m_ref[...]
        m_new = jnp.maximum(m_old, jnp.max(z, axis=0, keepdims=True) + cq2)
        p = jnp.exp2(z + (cq2 - m_new))
        alpha = jnp.exp2(m_old - m_new)
        l_ref[...] = alpha * l_ref[...] + jnp.sum(p, axis=0, keepdims=True)
        acc_ref[...] = alpha * acc_ref[...] + jnp.dot(vt[:, pl.ds(off, T)], p.astype(BF16),
                                                      preferred_element_type=F32)
        m_ref[...] = m_new

    logits(0, za)

    def pair(pp, carry):
        j = 2 * pp
        logits(j + 1, zb)
        softmax_pv(j, za, False)
        logits(j + 2, za)
        softmax_pv(j + 1, zb, False)
        return carry

    lax.fori_loop(0, i // 2, pair, 0)

    @pl.when(i % 2 == 0)
    def _():
        softmax_pv(i, za, True)

    @pl.when(i % 2 == 1)
    def _():
        logits(i, zb)
        softmax_pv(i - 1, za, False)
        softmax_pv(i, zb, True)

    o_ref[...] = (acc_ref[...] / l_ref[...]).T.astype(o_ref.dtype)


def fox_attention(q, kv, cum):
    s = q.shape[0]
    T = FOX_TILE
    D = FOX_HEAD_DIM
    cq = cum[:, :FOX_HEADS].T.reshape(FOX_HEADS, 1, s)
    ck = cum[:, :FOX_HEADS].T.reshape(FOX_HEADS, s, 1)
    return pl.pallas_call(
        _fox_kernel,
        out_shape=jax.ShapeDtypeStruct((s, ATT_DIM), BF16),
        grid=(FOX_HEADS, s // T),
        in_specs=[
            pl.BlockSpec((T, D), lambda h, i: (i, h)),
            pl.BlockSpec((s, D), lambda h, i: (0, h)),
            pl.BlockSpec((s, D), lambda h, i: (0, FOX_HEADS + h)),
            pl.BlockSpec((None, 1, T), lambda h, i: (h, 0, i)),
            pl.BlockSpec((None, s, 1), lambda h, i: (h, 0, 0)),
        ],
        out_specs=pl.BlockSpec((T, D), lambda h, i: (i, h)),
        scratch_shapes=[pltpu.VMEM((s, FOX_AUG), BF16), pltpu.VMEM((D, s), BF16),
                        pltpu.VMEM((T, FOX_AUG), BF16),
                        pltpu.VMEM((T, T), F32), pltpu.VMEM((T, T), F32),
                        pltpu.VMEM((1, T), F32), pltpu.VMEM((1, T), F32), pltpu.VMEM((D, T), F32)],
        compiler_params=_cparams(("arbitrary", "arbitrary")),
        name="fox_attn",
    )(q, kv, kv, cq, ck)


def fox_shared_kv(h, kv_norm, w_kvf, b_f):
    w_kv = w_kvf[:, :2 * ATT_DIM].astype(BF16)
    w_f = jnp.pad(w_kvf[:, 2 * ATT_DIM:], ((0, 0), (0, LANES - FOX_HEADS))).astype(BF16)
    kv = rms_mm(h, kv_norm, w_kv, BF16)
    f_raw = rms_mm(h, kv_norm, w_f, F32)
    b_pad = jnp.pad(b_f.astype(F32), (0, LANES - FOX_HEADS)).reshape(1, LANES)
    return kv, fox_cum(f_raw, b_pad)


def fox_layer(h, norm_w, w_q, w_o, kv, cum):
    q = rms_mm(h, norm_w, w_q.astype(BF16), BF16, out_scale=FOX_SCALE * LOG2E)
    o = fox_attention(q, kv, cum)
    return mm_res(o, w_o.astype(BF16), h)


ROUTE_TILE = 256
N_TOP = PEER_TOPK + 1
TOP_ROWS = 24
_CAND_NB = [N_TOP // (a + 1) for a in range(1, SUBLANES)]
CAND_ROWS = TOP_ROWS + SUBLANES * (SUBLANES - 1) + (TOP_ROWS - SUBLANES)


def _extract_top(x, out_ref, n):
    for k in range(n):
        m = jnp.max(x, axis=0, keepdims=True)
        out_ref[k:k + 1, :] = m
        x = jnp.where(x >= m, NEG_INF, x)


def _route_kernel(q_ref, keys_ref, s1p_ref, s2_ref, taup_ref, v1_ref, v2_ref, ct_ref):
    T = ROUTE_TILE

    def head(h, carry):
        v1_ref[...] = jnp.full_like(v1_ref, NEG_INF)
        v2_ref[...] = jnp.full_like(v2_ref, NEG_INF)
        ct_ref[...] = jnp.full_like(ct_ref, NEG_INF)
        col = pl.multiple_of(h * 2 * PEER_HALF, 2 * PEER_HALF)
        q1 = q_ref[:, pl.ds(col, PEER_HALF)]
        q2 = q_ref[:, pl.ds(col + PEER_HALF, PEER_HALF)]
        nt = (((1,), (1,)), ((), ()))
        s1 = lax.dot_general(keys_ref[h, 0], q1, nt, preferred_element_type=F32)
        s2 = lax.dot_general(keys_ref[h, 1], q2, nt, preferred_element_type=F32)
        _extract_top(s1, v1_ref, N_TOP)
        _extract_top(s2, v2_ref, N_TOP)
        v1 = v1_ref[...]
        v2 = v2_ref[...]
        pieces = [v1[0:1, :] + v2]
        rows8 = lax.broadcasted_iota(jnp.int32, (SUBLANES, T), 0)
        for a in range(1, SUBLANES):
            pieces.append(v1[a:a + 1, :] + jnp.where(rows8 < _CAND_NB[a - 1], v2[0:SUBLANES, :], NEG_INF))
        pieces.append(v1[SUBLANES:TOP_ROWS, :] + v2[0:1, :])
        cand = jnp.concatenate(pieces, axis=0)
        _extract_top(cand, ct_ref, N_TOP)
        m = ct_ref[0:1, :]
        tau = 0.5 * (ct_ref[PEER_TOPK - 1:PEER_TOPK, :] + ct_ref[PEER_TOPK:PEER_TOPK + 1, :])
        zsum = jnp.sum(jnp.where(cand >= tau, jnp.exp(cand - m), 0.0), axis=0, keepdims=True)
        shift = m + jnp.log(zsum)
        s1p_ref[h] = (s1 - shift) * LOG2E
        s2_ref[h] = s2 * LOG2E
        taup_ref[pl.ds(h, 1), :] = (tau - shift) * LOG2E
        return carry

    lax.fori_loop(0, PEER_HEADS, head, 0)


def peer_route(q, keys):
    s = q.shape[0]
    T = ROUTE_TILE
    big = pl.BlockSpec((PEER_HEADS, PEER_N_KEYS, T), lambda t: (0, 0, t))
    return pl.pallas_call(
        _route_kernel,
        out_shape=[jax.ShapeDtypeStruct((PEER_HEADS, PEER_N_KEYS, s), F32)] * 2
        + [jax.ShapeDtypeStruct((PEER_HEADS, s), F32)],
        grid=(s // T,),
        in_specs=[
            pl.BlockSpec((T, PEER_HEADS * 2 * PEER_HALF), lambda t: (t, 0)),
            pl.BlockSpec((PEER_HEADS, 2, PEER_N_KEYS, PEER_HALF), lambda t: (0, 0, 0, 0)),
        ],
        out_specs=[big, big, pl.BlockSpec((PEER_HEADS, T), lambda t: (0, t))],
        scratch_shapes=[pltpu.VMEM((TOP_ROWS, T), F32), pltpu.VMEM((TOP_ROWS, T), F32),
                        pltpu.VMEM((TOP_ROWS, T), F32)],
        compiler_params=_cparams(("arbitrary",)),
        name="peer_route",
    )(q, keys)


PEER_TOK_TILE = 512
PEER_EXP_TILE = 512
_SQRT_HALF = 0.7071067811865476


def _gelu(x):
    return 0.5 * x * (1.0 + lax.erf(x * _SQRT_HALF))


def _peer_kernel(xn_ref, u_ref, vt_ref, s1p_ref, s2_ref, taup_ref, o_ref,
                 at0, at1, ga0, ga1, acc_ref, *, n_e):
    s = pl.program_id(0)
    T = PEER_TOK_TILE
    n_i = PEER_EXP_TILE // PEER_N_KEYS
    n_items = pl.num_programs(0) - 2
    e_out = jnp.clip(s - 2, 0, n_items - 1) % n_e

    @pl.when(s == 0)
    def _():
        at1[...] = jnp.zeros_like(at1)
        ga0[...] = jnp.zeros_like(ga0)

    @pl.when(e_out == 0)
    def _():
        acc_ref[...] = jnp.zeros_like(acc_ref)

    def activations(at_w, part):
        cols = slice(part * (T // 2), (part + 1) * (T // 2))
        at_w[:, cols] = lax.dot_general(u_ref[...], xn_ref[cols, :], (((1,), (1,)), ((), ())),
                                        preferred_element_type=F32)

    def gate_chunk(at_r, ga_w, idx):
        tc, ii = divmod(idx, n_i)
        lanes = slice(tc * LANES, (tc + 1) * LANES)
        rows = slice(ii * PEER_N_KEYS, (ii + 1) * PEER_N_KEYS)
        g = jnp.zeros((PEER_N_KEYS, LANES), F32)
        for h in range(PEER_HEADS):
            ssum = s1p_ref[h, ii:ii + 1, lanes] + s2_ref[h, :, lanes]
            g = g + jnp.where(ssum >= taup_ref[h:h + 1, lanes], jnp.exp2(ssum), 0.0)
        ga_w[rows, lanes] = (g * _gelu(at_r[rows, lanes])).astype(BF16)

    def accumulate(ga_r, part):
        rows = slice(part * (D_MODEL // 4), (part + 1) * (D_MODEL // 4))
        acc_ref[rows, :] += jnp.dot(vt_ref[rows, :], ga_r[...], preferred_element_type=F32)

    def step(at_w, at_r, ga_w, ga_r):
        n_chunks = (T // LANES) * n_i
        mxu = [functools.partial(activations, at_w, 0), functools.partial(accumulate, ga_r, 0),
               functools.partial(accumulate, ga_r, 1), functools.partial(activations, at_w, 1),
               functools.partial(accumulate, ga_r, 2), functools.partial(accumulate, ga_r, 3)]
        for k in range(n_chunks):
            if (k * len(mxu)) % n_chunks < len(mxu):
                mxu[(k * len(mxu)) // n_chunks]()
            gate_chunk(at_r, ga_w, k)

    @pl.when(s % 2 == 0)
    def _():
        step(at0, at1, ga1, ga0)

    @pl.when(s % 2 == 1)
    def _():
        step(at1, at0, ga0, ga1)

    @pl.when((e_out == n_e - 1) & (s >= 2))
    def _():
        o_ref[...] = acc_ref[...].T


def peer_experts(xn, u, vt, s1p, s2, taup):
    s, d = xn.shape
    T = PEER_TOK_TILE
    Et = PEER_EXP_TILE
    n_i = Et // PEER_N_KEYS
    s1p = s1p.reshape(PEER_HEADS, PEER_N_KEYS // n_i, n_i, s)
    n_e = PEER_EXPERTS // Et
    n_items = (s // T) * n_e

    def item(step, lag):
        return jnp.clip(step - lag, 0, n_items - 1)

    tok = lambda step, lag: item(step, lag) // n_e
    exp = lambda step, lag: item(step, lag) % n_e
    return pl.pallas_call(
        functools.partial(_peer_kernel, n_e=n_e),
        out_shape=jax.ShapeDtypeStruct((s, d), F32),
        grid=(n_items + 2,),
        in_specs=[
            pl.BlockSpec((T, d), lambda i: (tok(i, 0), 0)),
            pl.BlockSpec((Et, d), lambda i: (exp(i, 0), 0)),
            pl.BlockSpec((d, Et), lambda i: (0, exp(i, 2))),
            pl.BlockSpec((PEER_HEADS, None, n_i, T), lambda i: (0, exp(i, 1), 0, tok(i, 1))),
            pl.BlockSpec((PEER_HEADS, PEER_N_KEYS, T), lambda i: (0, 0, tok(i, 1))),
            pl.BlockSpec((PEER_HEADS, T), lambda i: (0, tok(i, 1))),
        ],
        out_specs=pl.BlockSpec((T, d), lambda i: (tok(i, 2), 0)),
        scratch_shapes=[pltpu.VMEM((Et, T), F32), pltpu.VMEM((Et, T), F32),
                        pltpu.VMEM((Et, T), BF16), pltpu.VMEM((Et, T), BF16),
                        pltpu.VMEM((d, T), F32)],
        compiler_params=_cparams(("arbitrary",)),
        name="peer_experts",
    )(xn, u, vt, s1p, s2, taup)


def peer_layer(h, norm_w, w_q, sub_keys, u_tab, v_tab):
    q, xn = rms_mm(h, norm_w, w_q.astype(BF16), BF16, emit_xn=True)
    s1p, s2, taup = peer_route(q, sub_keys.astype(BF16))
    return peer_experts(xn, u_tab.astype(BF16), v_tab.T.astype(BF16), s1p, s2, taup)


def _trunk(x, p, a_norm, a_in_proj, a_conv_w, a_conv_b, a_dt_bias, a_A_log, a_D, a_gnorm, a_out_proj,
           kv_norm, w_kvf, b_f, b_norm, b_wq, b_wo, c_norm, c_wq, c_subkeys, c_u, c_v,
           e_norm, e_wg, e_bg, e_wp, f_norm):
    depth = p.shape[0]
    n_a = a_norm.shape[0]
    h = x[0]
    shared = None
    for i in range(depth):
        if i < n_a:
            h = mamba_layer(h, a_norm[i], a_in_proj[i], a_conv_w[i], a_conv_b[i], a_dt_bias[i],
                            a_A_log[i], a_D[i], a_gnorm[i], a_out_proj[i])
        else:
            j = i - n_a
            h = fox_layer(h, b_norm[j], b_wq[j], b_wo[j], *shared)
        y = peer_layer(h, c_norm[i], c_wq[i], c_subkeys[i], c_u[i], c_v[i])
        h = ple_layer(h, y, e_norm[i], e_wg[i].astype(BF16), e_bg[i], p[i, 0].astype(BF16),
                      e_wp[i].astype(BF16))
        if i == n_a - 1:
            shared = fox_shared_kv(h, kv_norm, w_kvf, b_f)
    return final_norm(h, f_norm)[None]


def kernel(x, p, a_norm, a_in_proj, a_conv_w, a_conv_b, a_dt_bias, a_A_log, a_D, a_gnorm, a_out_proj, kv_norm, w_kvf, b_f, b_norm, b_wq, b_wo, c_norm, c_wq, c_subkeys, c_u, c_v, e_norm, e_wg, e_bg, e_wp, f_norm):
    assert x.shape[0] == 1
    return _trunk(x, p, a_norm, a_in_proj, a_conv_w, a_conv_b, a_dt_bias, a_A_log, a_D, a_gnorm,
                  a_out_proj, kv_norm, w_kvf, b_f, b_norm, b_wq, b_wo, c_norm, c_wq, c_subkeys,
                  c_u, c_v, e_norm, e_wg, e_bg, e_wp, f_norm)
```

```python
import functools
import math

import jax
import jax.numpy as jnp
from jax import lax
from jax.experimental import pallas as pl
from jax.experimental.pallas import tpu as pltpu

F32 = jnp.float32
BF16 = jnp.bfloat16
NEG_INF = float("-inf")

D_MODEL = 2048
RMS_EPS = 1e-6
GATED_NORM_EPS = 1e-5

D_INNER = 4096
SSM_HEAD_DIM = 64
SSM_HEADS = 64
SSM_GROUPS = 8
SSM_HEADS_PER_GROUP = SSM_HEADS // SSM_GROUPS
SSM_STATE = 128
D_CONV = 4
CONV_DIM = D_INNER + 2 * SSM_GROUPS * SSM_STATE
SSD_CHUNK = 256
GROUP_CH = D_INNER // SSM_GROUPS

FOX_HEADS = 16
FOX_HEAD_DIM = 128
ATT_DIM = FOX_HEADS * FOX_HEAD_DIM
FOX_SCALE = FOX_HEAD_DIM ** -0.5

PEER_HEADS = 8
PEER_N_KEYS = 128
PEER_EXPERTS = PEER_N_KEYS * PEER_N_KEYS
PEER_TOPK = 16
PEER_HALF = 128
PLE_DIM = 256

LANES = 128
SUBLANES = 8
VMEM_LIMIT = 56 * 1024 * 1024


def _cparams(sem):
    return pltpu.CompilerParams(dimension_semantics=sem, vmem_limit_bytes=VMEM_LIMIT)


def _rms(x, w, eps):
    return x * lax.rsqrt(jnp.mean(x * x, axis=-1, keepdims=True) + eps) * w


def _rms_mm_kernel(x_ref, nw_ref, w_ref, *rest, out_scale, emit_xn):
    if emit_xn:
        o_ref, xn_out_ref, xn_ref = rest
    else:
        o_ref, xn_ref = rest

    @pl.when(pl.program_id(1) == 0)
    def _():
        xn = _rms(x_ref[...], nw_ref[...], RMS_EPS).astype(BF16)
        xn_ref[...] = xn
        if emit_xn:
            xn_out_ref[...] = xn

    acc = jnp.dot(xn_ref[...], w_ref[...], preferred_element_type=F32)
    if out_scale != 1.0:
        acc = acc * out_scale
    o_ref[...] = acc.astype(o_ref.dtype)


def rms_mm(x, nw, w, out_dtype, *, tm=1024, tn=512, out_scale=1.0, emit_xn=False):
    s, d = x.shape
    n = w.shape[1]
    tn = min(tn, n)
    assert s % tm == 0 and n % tn == 0
    out_shape = [jax.ShapeDtypeStruct((s, n), out_dtype)]
    out_specs = [pl.BlockSpec((tm, tn), lambda i, j: (i, j))]
    if emit_xn:
        out_shape.append(jax.ShapeDtypeStruct((s, d), BF16))
        out_specs.append(pl.BlockSpec((tm, d), lambda i, j: (i, 0)))
    res = pl.pallas_call(
        functools.partial(_rms_mm_kernel, out_scale=out_scale, emit_xn=emit_xn),
        out_shape=out_shape,
        grid=(s // tm, n // tn),
        in_specs=[
            pl.BlockSpec((tm, d), lambda i, j: (i, 0)),
            pl.BlockSpec((1, d), lambda i, j: (0, 0)),
            pl.BlockSpec((d, tn), lambda i, j: (0, j)),
        ],
        out_specs=out_specs,
        scratch_shapes=[pltpu.VMEM((tm, d), BF16)],
        compiler_params=_cparams(("arbitrary", "arbitrary")),
        name="rms_mm",
    )(x, nw.reshape(1, d), w)
    return res if emit_xn else res[0]


def _mm_res_kernel(a_ref, w_ref, r_ref, o_ref):
    o_ref[...] = r_ref[...] + jnp.dot(a_ref[...], w_ref[...], preferred_element_type=F32)


def mm_res(a, w, res, *, tm=512, tn=512):
    s, k = a.shape
    n = w.shape[1]
    return pl.pallas_call(
        _mm_res_kernel,
        out_shape=jax.ShapeDtypeStruct((s, n), F32),
        grid=(s // tm, n // tn),
        in_specs=[
            pl.BlockSpec((tm, k), lambda i, j: (i, 0)),
            pl.BlockSpec((k, tn), lambda i, j: (0, j)),
            pl.BlockSpec((tm, tn), lambda i, j: (i, j)),
        ],
        out_specs=pl.BlockSpec((tm, tn), lambda i, j: (i, j)),
        compiler_params=_cparams(("arbitrary", "arbitrary")),
        name="mm_res",
    )(a, w, res)


def _ple_kernel(h_ref, y_ref, nw_ref, wg_ref, bg_ref, p_ref, wp_ref, o_ref, h1_ref, xn_ref, *, tn):
    j = pl.program_id(1)

    @pl.when(j == 0)
    def _():
        h1 = h_ref[...] + y_ref[...]
        h1_ref[...] = h1
        xn_ref[...] = _rms(h1, nw_ref[...], RMS_EPS).astype(BF16)

    gate = jax.nn.sigmoid(jnp.dot(xn_ref[...], wg_ref[...], preferred_element_type=F32) + bg_ref[...])
    proj = jnp.dot(p_ref[...], wp_ref[...], preferred_element_type=F32)
    col = pl.multiple_of(j * tn, tn)
    o_ref[...] = h1_ref[:, pl.ds(col, tn)] + gate * proj


def ple_layer(h, y, nw, wg, bg, p, wp, *, tm=512, tn=512):
    s, d = h.shape
    pd = p.shape[1]
    return pl.pallas_call(
        functools.partial(_ple_kernel, tn=tn),
        out_shape=jax.ShapeDtypeStruct((s, d), F32),
        grid=(s // tm, d // tn),
        in_specs=[
            pl.BlockSpec((tm, d), lambda i, j: (i, 0)),
            pl.BlockSpec((tm, d), lambda i, j: (i, 0)),
            pl.BlockSpec((1, d), lambda i, j: (0, 0)),
            pl.BlockSpec((d, tn), lambda i, j: (0, j)),
            pl.BlockSpec((1, tn), lambda i, j: (0, j)),
            pl.BlockSpec((tm, pd), lambda i, j: (i, 0)),
            pl.BlockSpec((pd, tn), lambda i, j: (0, j)),
        ],
        out_specs=pl.BlockSpec((tm, tn), lambda i, j: (i, j)),
        scratch_shapes=[pltpu.VMEM((tm, d), F32), pltpu.VMEM((tm, d), BF16)],
        compiler_params=_cparams(("arbitrary", "arbitrary")),
        name="ple",
    )(h, y, nw.reshape(1, d), wg, bg.reshape(1, d), p, wp)


def _final_norm_kernel(x_ref, w_ref, o_ref):
    o_ref[...] = _rms(x_ref[...], w_ref[...], RMS_EPS)


def final_norm(x, w, *, tm=512):
    s, d = x.shape
    return pl.pallas_call(
        _final_norm_kernel,
        out_shape=jax.ShapeDtypeStruct((s, d), F32),
        grid=(s // tm,),
        in_specs=[pl.BlockSpec((tm, d), lambda i: (i, 0)), pl.BlockSpec((1, d), lambda i: (0, 0))],
        out_specs=pl.BlockSpec((tm, d), lambda i: (i, 0)),
        compiler_params=_cparams(("arbitrary",)),
        name="final_norm",
    )(x, w.reshape(1, d))


def _tril_ones(n):
    r = lax.broadcasted_iota(jnp.int32, (n, n), 0)
    c = lax.broadcasted_iota(jnp.int32, (n, n), 1)
    return (c <= r).astype(F32)


N_SPLIT = 3
N_SCALES = 3


def _split3(v):
    hi = v.astype(BF16)
    r1 = v - hi.astype(F32)
    mid = r1.astype(BF16)
    lo = (r1 - mid.astype(F32)).astype(BF16)
    return hi, mid, lo


def _dt_prep_kernel(raw_ref, bias_ref, alog_ref, ac_ref, pk_ref):
    L = SSD_CHUNK
    w = raw_ref.shape[1]
    x = raw_ref[...] + bias_ref[...]
    dt = jnp.maximum(x, 0.0) + jnp.log1p(jnp.exp(-jnp.abs(x)))
    a = -jnp.exp(alog_ref[...])
    ac = jnp.dot(_tril_ones(L), dt * a, preferred_element_type=F32, precision=lax.Precision.HIGHEST)
    ac_ref[...] = ac
    scales = (dt, jnp.exp(ac), jnp.exp(ac[L - 1:L, :] - ac) * dt)
    for q, v in enumerate(scales):
        for k, piece in enumerate(_split3(v)):
            col = (q * N_SPLIT + k) * w
            pk_ref[:, col:col + w] = piece


def dt_prep(dt_raw, dt_bias, a_log):
    s, w = dt_raw.shape
    row = pl.BlockSpec((SSD_CHUNK, w), lambda c: (c, 0))
    vec = pl.BlockSpec((1, w), lambda c: (0, 0))
    wide = N_SCALES * N_SPLIT * w
    return pl.pallas_call(
        _dt_prep_kernel,
        out_shape=[jax.ShapeDtypeStruct((s, w), F32), jax.ShapeDtypeStruct((s, wide), BF16)],
        grid=(s // SSD_CHUNK,),
        in_specs=[row, vec, vec],
        out_specs=[row, pl.BlockSpec((SSD_CHUNK, wide), lambda c: (c, 0))],
        compiler_params=_cparams(("arbitrary",)),
        name="dt_prep",
    )(dt_raw, dt_bias, a_log)


def _silu(x):
    return x * jax.nn.sigmoid(x)


def _ssd_kernel(z_ref, x_ref, b_ref, c_ref, wx_ref, wb_ref, wc_ref, bx_ref, bb_ref, bc_ref,
                pk_ref, ac_ref, act_ref, dexp_ref, gw_ref, expand_ref, o_ref,
                xbuf, bbuf, cbuf, state_ref, yint_ref):
    L = SSD_CHUNK
    P = SSM_HEAD_DIM
    R = SSM_HEADS_PER_GROUP
    H = D_CONV - 1
    c_idx = pl.program_id(1)

    @pl.when(c_idx == 0)
    def _():
        xbuf[0:SUBLANES, :] = jnp.zeros((SUBLANES, GROUP_CH), F32)
        bbuf[0:SUBLANES, :] = jnp.zeros((SUBLANES, SSM_STATE), F32)
        cbuf[0:SUBLANES, :] = jnp.zeros((SUBLANES, SSM_STATE), F32)
        state_ref[...] = jnp.zeros_like(state_ref)

    def conv(buf, cur_ref, w_ref, bias_ref):
        buf[SUBLANES:SUBLANES + L, :] = cur_ref[...]
        y = bias_ref[...] + w_ref[D_CONV - 1:D_CONV, :] * cur_ref[...]
        for k in range(H):
            y = y + w_ref[k:k + 1, :] * buf[pl.ds(SUBLANES - H + k, L), :]
        buf[0:SUBLANES, :] = cur_ref[L - SUBLANES:L, :]
        return _silu(y)

    xg = conv(xbuf, x_ref, wx_ref, bx_ref)
    bm = conv(bbuf, b_ref, wb_ref, bb_ref)
    cm = conv(cbuf, c_ref, wc_ref, bc_ref)
    bb = bm.astype(BF16)
    cb16 = cm.astype(BF16)

    ac8 = ac_ref[...]
    act8 = act_ref[...]

    scales = jnp.dot(pk_ref[...], expand_ref[...], preferred_element_type=F32)
    dt_e = scales[:, 0:GROUP_CH]
    eac_e = scales[:, GROUP_CH:2 * GROUP_CH]
    w_e = scales[:, 2 * GROUP_CH:3 * GROUP_CH]
    elast_e = eac_e[L - 1:L, :]

    xdt = (xg * dt_e).astype(BF16)
    xw = (xg * w_e).astype(BF16)

    cbm = lax.dot_general(cb16, bb, (((1,), (1,)), ((), ())), preferred_element_type=F32)
    row = lax.broadcasted_iota(jnp.int32, (L, L), 0)
    col = lax.broadcasted_iota(jnp.int32, (L, L), 1)
    tril = col <= row
    for r in range(R):
        seg = ac8[:, r:r + 1] - act8[r:r + 1, :]
        decay = jnp.exp(jnp.where(tril, seg, NEG_INF))
        m = (cbm * decay).astype(BF16)
        yint_ref[:, r * P:(r + 1) * P] = jnp.dot(m, xdt[:, r * P:(r + 1) * P],
                                                  preferred_element_type=F32)

    state = state_ref[...]
    y_inter = jnp.dot(cb16, state.astype(BF16), preferred_element_type=F32) * eac_e
    bt = bm.T.astype(BF16)
    state_ref[...] = state * elast_e + jnp.dot(bt, xw, preferred_element_type=F32)

    y = yint_ref[...] + y_inter + dexp_ref[...] * xg
    gf = y * _silu(z_ref[...])
    o_ref[...] = _rms(gf, gw_ref[...], GATED_NORM_EPS).astype(o_ref.dtype)


def ssd_mixer(zx, conv_w, conv_b, pkg, acg, act, d_exp, gnorm_w):
    s = zx.shape[0]
    L = SSD_CHUNK
    G = SSM_GROUPS
    R = SSM_HEADS_PER_GROUP
    gn = SSM_STATE
    xoff = D_INNER // GROUP_CH
    boff = (2 * D_INNER) // gn
    coff = boff + G
    wboff = D_INNER // gn
    wcoff = wboff + G
    n_pk = N_SCALES * N_SPLIT * R
    pk_row = jnp.arange(n_pk)
    out_col = jnp.arange(N_SCALES * GROUP_CH)
    expand = ((pk_row[:, None] // (N_SPLIT * R) == out_col[None, :] // GROUP_CH)
              & (pk_row[:, None] % R == (out_col[None, :] % GROUP_CH) // SSM_HEAD_DIM)).astype(BF16)
    in_specs = [
        pl.BlockSpec((L, GROUP_CH), lambda g, c: (c, g)),
        pl.BlockSpec((L, GROUP_CH), lambda g, c: (c, xoff + g)),
        pl.BlockSpec((L, gn), lambda g, c: (c, boff + g)),
        pl.BlockSpec((L, gn), lambda g, c: (c, coff + g)),
        pl.BlockSpec((D_CONV, GROUP_CH), lambda g, c: (0, g)),
        pl.BlockSpec((D_CONV, gn), lambda g, c: (0, wboff + g)),
        pl.BlockSpec((D_CONV, gn), lambda g, c: (0, wcoff + g)),
        pl.BlockSpec((1, GROUP_CH), lambda g, c: (0, g)),
        pl.BlockSpec((1, gn), lambda g, c: (0, wboff + g)),
        pl.BlockSpec((1, gn), lambda g, c: (0, wcoff + g)),
        pl.BlockSpec((None, L, n_pk), lambda g, c: (g, c, 0)),
        pl.BlockSpec((None, L, R), lambda g, c: (g, c, 0)),
        pl.BlockSpec((None, R, L), lambda g, c: (g, 0, c)),
        pl.BlockSpec((1, GROUP_CH), lambda g, c: (0, g)),
        pl.BlockSpec((1, GROUP_CH), lambda g, c: (0, g)),
        pl.BlockSpec((n_pk, N_SCALES * GROUP_CH), lambda g, c: (0, 0)),
    ]
    return pl.pallas_call(
        _ssd_kernel,
        out_shape=jax.ShapeDtypeStruct((s, D_INNER), BF16),
        grid=(G, s // L),
        in_specs=in_specs,
        out_specs=pl.BlockSpec((L, GROUP_CH), lambda g, c: (c, g)),
        scratch_shapes=[
            pltpu.VMEM((L + SUBLANES, GROUP_CH), F32),
            pltpu.VMEM((L + SUBLANES, gn), F32),
            pltpu.VMEM((L + SUBLANES, gn), F32),
            pltpu.VMEM((gn, GROUP_CH), F32),
            pltpu.VMEM((L, GROUP_CH), F32),
        ],
        compiler_params=_cparams(("arbitrary", "arbitrary")),
        name="ssd",
    )(zx, zx, zx, zx, conv_w, conv_w, conv_w, conv_b, conv_b, conv_b,
      pkg, acg, act, d_exp, gnorm_w, expand)


def mamba_layer(h, norm_w, w_in, conv_w, conv_b, dt_bias, a_log, d_skip, gnorm_w, w_out):
    s = h.shape[0]
    n_zx = D_INNER + CONV_DIM
    w_zx = w_in[:, :n_zx].astype(BF16)
    w_dt = jnp.pad(w_in[:, n_zx:], ((0, 0), (0, LANES - SSM_HEADS))).astype(BF16)
    zx = rms_mm(h, norm_w, w_zx, F32)
    dt_raw = rms_mm(h, norm_w, w_dt, F32)
    pad = lambda v: jnp.pad(v.astype(F32), (0, LANES - SSM_HEADS)).reshape(1, LANES)
    ac, pk = dt_prep(dt_raw, pad(dt_bias), pad(a_log))
    acg = ac[:, :SSM_HEADS].reshape(s, SSM_GROUPS, SSM_HEADS_PER_GROUP).transpose(1, 0, 2)
    act = acg.transpose(0, 2, 1)
    pkg = pk.reshape(s, N_SCALES * N_SPLIT, LANES)[:, :, :SSM_HEADS]
    pkg = pkg.reshape(s, N_SCALES * N_SPLIT, SSM_GROUPS, SSM_HEADS_PER_GROUP).transpose(2, 0, 1, 3)
    pkg = pkg.reshape(SSM_GROUPS, s, N_SCALES * N_SPLIT * SSM_HEADS_PER_GROUP)
    d_exp = jnp.repeat(d_skip.astype(F32), SSM_HEAD_DIM).reshape(1, D_INNER)
    y = ssd_mixer(zx, conv_w, conv_b.reshape(1, CONV_DIM), pkg, acg, act, d_exp,
                  gnorm_w.reshape(1, D_INNER))
    return mm_res(y, w_out.astype(BF16), h)


CUM_TILE = 256


def _cum_kernel(f_ref, b_ref, o_ref, carry_ref):
    @pl.when(pl.program_id(0) == 0)
    def _():
        carry_ref[...] = jnp.zeros_like(carry_ref)

    log_f = jax.nn.log_sigmoid(f_ref[...] + b_ref[...])
    cum = jnp.dot(_tril_ones(CUM_TILE), log_f, preferred_element_type=F32,
                  precision=lax.Precision.HIGHEST) + carry_ref[...]
    o_ref[...] = cum
    carry_ref[...] = cum[CUM_TILE - 1:CUM_TILE, :]


def fox_cum(f_raw, b_f):
    s, w = f_raw.shape
    return pl.pallas_call(
        _cum_kernel,
        out_shape=jax.ShapeDtypeStruct((s, w), F32),
        grid=(s // CUM_TILE,),
        in_specs=[pl.BlockSpec((CUM_TILE, w), lambda i: (i, 0)), pl.BlockSpec((1, w), lambda i: (0, 0))],
        out_specs=pl.BlockSpec((CUM_TILE, w), lambda i: (i, 0)),
        scratch_shapes=[pltpu.VMEM((1, w), F32)],
        compiler_params=_cparams(("arbitrary",)),
        name="fox_cum",
    )(f_raw, b_f)


FOX_TILE = 512


FOX_AUG = 2 * FOX_HEAD_DIM
LOG2E = 1.4426950408889634


def _fox_kernel(q_ref, k_ref, v_ref, cq_ref, ck_ref, o_ref, kaug, vt, qaug, za, zb, m_ref, l_ref, acc_ref):
    i = pl.program_id(1)
    T = FOX_TILE
    D = FOX_HEAD_DIM
    lane = lax.broadcasted_iota(jnp.int32, (1, D), 1)

    @pl.when(i == 0)
    def _():
        c2 = ck_ref[...] * (-LOG2E)
        hi = c2.astype(BF16)
        r1 = c2 - hi.astype(F32)
        mid = r1.astype(BF16)
        lo = (r1 - mid.astype(F32)).astype(BF16)
        aug = jnp.where(lane == 0, hi.astype(F32),
                        jnp.where(lane == 1, mid.astype(F32),
                                  jnp.where(lane == 2, lo.astype(F32), 0.0)))
        kaug[:, 0:D] = k_ref[...]
        kaug[:, D:FOX_AUG] = aug.astype(BF16)
        vt[...] = v_ref[...].astype(F32).T.astype(BF16)

    qaug[:, 0:D] = q_ref[...]
    qaug[:, D:FOX_AUG] = jnp.broadcast_to(jnp.where(lane < 3, 1.0, 0.0), (T, D)).astype(BF16)
    m_ref[...] = jnp.full_like(m_ref, NEG_INF)
    l_ref[...] = jnp.zeros_like(l_ref)
    acc_ref[...] = jnp.zeros_like(acc_ref)
    cq2 = cq_ref[...] * LOG2E
    nt = (((1,), (1,)), ((), ()))

    def logits(j, z_ref):
        off = pl.multiple_of(j * T, T)
        z_ref[...] = lax.dot_general(kaug[pl.ds(off, T), :], qaug[...], nt,
                                     preferred_element_type=F32)

    def softmax_pv(j, z_ref, masked):
        off = pl.multiple_of(j * T, T)
        z = z_ref[...]
        if masked:
            key = lax.broadcasted_iota(jnp.int32, (T, T), 0)
            qry = lax.broadcasted_iota(jnp.int32, (T, T), 1)
            z = jnp.where(key <= qry, z, NEG_INF)
        m_old = m_ref[...]
        m_new = jnp.maximum(m_old, jnp.max(z, axis=0, keepdims=True) + cq2)
        p = jnp.exp2(z + (cq2 - m_new))
        alpha = jnp.exp2(m_old - m_new)
        l_ref[...] = alpha * l_ref[...] + jnp.sum(p, axis=0, keepdims=True)
        acc_ref[...] = alpha * acc_ref[...] + jnp.dot(vt[:, pl.ds(off, T)], p.astype(BF16),
                                                      preferred_element_type=F32)
        m_ref[...] = m_new

    logits(0, za)

    def pair(pp, carry):
        j = 2 * pp
        logits(j + 1, zb)
        softmax_pv(j, za, False)
        logits(j + 2, za)
        softmax_pv(j + 1, zb, False)
        return carry

    lax.fori_loop(0, i // 2, pair, 0)

    @pl.when(i % 2 == 0)
    def _():
        softmax_pv(i, za, True)

    @pl.when(i % 2 == 1)
    def _():
        logits(i, zb)
        softmax_pv(i - 1, za, False)
        softmax_pv(i, zb, True)

    o_ref[...] = (acc_ref[...] / l_ref[...]).T.astype(o_ref.dtype)


def fox_attention(q, kv, cum):
    s = q.shape[0]
    T = FOX_TILE
    D = FOX_HEAD_DIM
    cq = cum[:, :FOX_HEADS].T.reshape(FOX_HEADS, 1, s)
    ck = cum[:, :FOX_HEADS].T.reshape(FOX_HEADS, s, 1)
    return pl.pallas_call(
        _fox_kernel,
        out_shape=jax.ShapeDtypeStruct((s, ATT_DIM), BF16),
        grid=(FOX_HEADS, s // T),
        in_specs=[
            pl.BlockSpec((T, D), lambda h, i: (i, h)),
            pl.BlockSpec((s, D), lambda h, i: (0, h)),
            pl.BlockSpec((s, D), lambda h, i: (0, FOX_HEADS + h)),
            pl.BlockSpec((None, 1, T), lambda h, i: (h, 0, i)),
            pl.BlockSpec((None, s, 1), lambda h, i: (h, 0, 0)),
        ],
        out_specs=pl.BlockSpec((T, D), lambda h, i: (i, h)),
        scratch_shapes=[pltpu.VMEM((s, FOX_AUG), BF16), pltpu.VMEM((D, s), BF16),
                        pltpu.VMEM((T, FOX_AUG), BF16),
                        pltpu.VMEM((T, T), F32), pltpu.VMEM((T, T), F32),
                        pltpu.VMEM((1, T), F32), pltpu.VMEM((1, T), F32), pltpu.VMEM((D, T), F32)],
        compiler_params=_cparams(("arbitrary", "arbitrary")),
        name="fox_attn",
    )(q, kv, kv, cq, ck)


def fox_shared_kv(h, kv_norm, w_kvf, b_f):
    w_kv = w_kvf[:, :2 * ATT_DIM].astype(BF16)
    w_f = jnp.pad(w_kvf[:, 2 * ATT_DIM:], ((0, 0), (0, LANES - FOX_HEADS))).astype(BF16)
    kv = rms_mm(h, kv_norm, w_kv, BF16)
    f_raw = rms_mm(h, kv_norm, w_f, F32)
    b_pad = jnp.pad(b_f.astype(F32), (0, LANES - FOX_HEADS)).reshape(1, LANES)
    return kv, fox_cum(f_raw, b_pad)


def fox_layer(h, norm_w, w_q, w_o, kv, cum):
    q = rms_mm(h, norm_w, w_q.astype(BF16), BF16, out_scale=FOX_SCALE * LOG2E)
    o = fox_attention(q, kv, cum)
    return mm_res(o, w_o.astype(BF16), h)


ROUTE_TILE = 256
N_TOP = PEER_TOPK + 1
TOP_ROWS = 24
_CAND_NB = [N_TOP // (a + 1) for a in range(1, SUBLANES)]
CAND_ROWS = TOP_ROWS + SUBLANES * (SUBLANES - 1) + (TOP_ROWS - SUBLANES)


def _extract_top(x, out_ref, n):
    for k in range(n):
        m = jnp.max(x, axis=0, keepdims=True)
        out_ref[k:k + 1, :] = m
        x = jnp.where(x >= m, NEG_INF, x)


def _route_kernel(q_ref, keys_ref, s1p_ref, s2_ref, taup_ref, v1_ref, v2_ref, ct_ref):
    T = ROUTE_TILE

    def head(h, carry):
        v1_ref[...] = jnp.full_like(v1_ref, NEG_INF)
        v2_ref[...] = jnp.full_like(v2_ref, NEG_INF)
        ct_ref[...] = jnp.full_like(ct_ref, NEG_INF)
        col = pl.multiple_of(h * 2 * PEER_HALF, 2 * PEER_HALF)
        q1 = q_ref[:, pl.ds(col, PEER_HALF)]
        q2 = q_ref[:, pl.ds(col + PEER_HALF, PEER_HALF)]
        nt = (((1,), (1,)), ((), ()))
        s1 = lax.dot_general(keys_ref[h, 0], q1, nt, preferred_element_type=F32)
        s2 = lax.dot_general(keys_ref[h, 1], q2, nt, preferred_element_type=F32)
        _extract_top(s1, v1_ref, N_TOP)
        _extract_top(s2, v2_ref, N_TOP)
        v1 = v1_ref[...]
        v2 = v2_ref[...]
        pieces = [v1[0:1, :] + v2]
        rows8 = lax.broadcasted_iota(jnp.int32, (SUBLANES, T), 0)
        for a in range(1, SUBLANES):
            pieces.append(v1[a:a + 1, :] + jnp.where(rows8 < _CAND_NB[a - 1], v2[0:SUBLANES, :], NEG_INF))
        pieces.append(v1[SUBLANES:TOP_ROWS, :] + v2[0:1, :])
        cand = jnp.concatenate(pieces, axis=0)
        _extract_top(cand, ct_ref, N_TOP)
        m = ct_ref[0:1, :]
        tau = 0.5 * (ct_ref[PEER_TOPK - 1:PEER_TOPK, :] + ct_ref[PEER_TOPK:PEER_TOPK + 1, :])
        zsum = jnp.sum(jnp.where(cand >= tau, jnp.exp(cand - m), 0.0), axis=0, keepdims=True)
        shift = m + jnp.log(zsum)
        s1p_ref[h] = (s1 - shift) * LOG2E
        s2_ref[h] = s2 * LOG2E
        taup_ref[pl.ds(h, 1), :] = (tau - shift) * LOG2E
        return carry

    lax.fori_loop(0, PEER_HEADS, head, 0)


def peer_route(q, keys):
    s = q.shape[0]
    T = ROUTE_TILE
    big = pl.BlockSpec((PEER_HEADS, PEER_N_KEYS, T), lambda t: (0, 0, t))
    return pl.pallas_call(
        _route_kernel,
        out_shape=[jax.ShapeDtypeStruct((PEER_HEADS, PEER_N_KEYS, s), F32)] * 2
        + [jax.ShapeDtypeStruct((PEER_HEADS, s), F32)],
        grid=(s // T,),
        in_specs=[
            pl.BlockSpec((T, PEER_HEADS * 2 * PEER_HALF), lambda t: (t, 0)),
            pl.BlockSpec((PEER_HEADS, 2, PEER_N_KEYS, PEER_HALF), lambda t: (0, 0, 0, 0)),
        ],
        out_specs=[big, big, pl.BlockSpec((PEER_HEADS, T), lambda t: (0, t))],
        scratch_shapes=[pltpu.VMEM((TOP_ROWS, T), F32), pltpu.VMEM((TOP_ROWS, T), F32),
                        pltpu.VMEM((TOP_ROWS, T), F32)],
        compiler_params=_cparams(("arbitrary",)),
        name="peer_route",
    )(q, keys)


PEER_TOK_TILE = 512
PEER_EXP_TILE = 512
_SQRT_HALF = 0.7071067811865476


def _gelu(x):
    return 0.5 * x * (1.0 + lax.erf(x * _SQRT_HALF))


def _peer_kernel(xn_ref, u_ref, vt_ref, s1p_ref, s2_ref, taup_ref, o_ref,
                 at0, at1, ga, acc_ref, *, n_e):
    s = pl.program_id(0)
    T = PEER_TOK_TILE
    n_i = PEER_EXP_TILE // PEER_N_KEYS
    n_items = pl.num_programs(0) - 1
    e_out = jnp.clip(s - 1, 0, n_items - 1) % n_e

    @pl.when(s == 0)
    def _():
        at1[...] = jnp.zeros_like(at1)

    @pl.when(e_out == 0)
    def _():
        acc_ref[...] = jnp.zeros_like(acc_ref)

    HT = T // 2

    def gate_chunk(at_r, ga_w, lane0, ii):
        lanes = pl.ds(lane0, LANES)
        rb = 2 * SUBLANES
        s1b = [jnp.broadcast_to(s1p_ref[h, ii:ii + 1, lanes], (SUBLANES, LANES)) for h in range(PEER_HEADS)]
        tb = [jnp.broadcast_to(taup_ref[h:h + 1, lanes], (SUBLANES, LANES)) for h in range(PEER_HEADS)]
        for blk in range(PEER_N_KEYS // rb):
            parts = []
            for sub in range(rb // SUBLANES):
                keys = slice(blk * rb + sub * SUBLANES, blk * rb + (sub + 1) * SUBLANES)
                g = jnp.zeros((SUBLANES, LANES), F32)
                for h in range(PEER_HEADS):
                    ssum = s1b[h] + s2_ref[h, keys, lanes]
                    g = g + jnp.where(ssum >= tb[h], jnp.exp2(ssum), 0.0)
                parts.append(g)
            rows = slice(ii * PEER_N_KEYS + blk * rb, ii * PEER_N_KEYS + (blk + 1) * rb)
            g16 = jnp.concatenate(parts, axis=0)
            ga_w[rows, lanes] = (g16 * _gelu(at_r[rows, lanes])).astype(BF16)

    def step(at_w, at_r):
        for half in range(T // HT):
            cols = slice(half * HT, (half + 1) * HT)
            for sub in range(HT // LANES):
                for ii in range(n_i):
                    gate_chunk(at_r, ga, half * HT + sub * LANES, ii)
            acc_ref[:, cols] += jnp.dot(vt_ref[...], ga[:, cols], preferred_element_type=F32)
            if half == 0:
                at_w[...] = lax.dot_general(u_ref[...], xn_ref[...], (((1,), (1,)), ((), ())),
                                            preferred_element_type=F32)

    @pl.when(s % 2 == 0)
    def _():
        step(at0, at1)

    @pl.when(s % 2 == 1)
    def _():
        step(at1, at0)

    @pl.when((e_out == n_e - 1) & (s >= 1))
    def _():
        o_ref[...] = acc_ref[...].T


def peer_experts(xn, u, vt, s1p, s2, taup):
    s, d = xn.shape
    T = PEER_TOK_TILE
    Et = PEER_EXP_TILE
    n_i = Et // PEER_N_KEYS
    s1p = s1p.reshape(PEER_HEADS, PEER_N_KEYS // n_i, n_i, s)
    n_e = PEER_EXPERTS // Et
    n_items = (s // T) * n_e

    def item(step, lag):
        return jnp.clip(step - lag, 0, n_items - 1)

    tok = lambda step, lag: item(step, lag) // n_e
    exp = lambda step, lag: item(step, lag) % n_e
    return pl.pallas_call(
        functools.partial(_peer_kernel, n_e=n_e),
        out_shape=jax.ShapeDtypeStruct((s, d), F32),
        grid=(n_items + 1,),
        in_specs=[
            pl.BlockSpec((T, d), lambda i: (tok(i, 0), 0)),
            pl.BlockSpec((Et, d), lambda i: (exp(i, 0), 0)),
            pl.BlockSpec((d, Et), lambda i: (0, exp(i, 1))),
            pl.BlockSpec((PEER_HEADS, None, n_i, T), lambda i: (0, exp(i, 1), 0, tok(i, 1))),
            pl.BlockSpec((PEER_HEADS, PEER_N_KEYS, T), lambda i: (0, 0, tok(i, 1))),
            pl.BlockSpec((PEER_HEADS, T), lambda i: (0, tok(i, 1))),
        ],
        out_specs=pl.BlockSpec((T, d), lambda i: (tok(i, 1), 0)),
        scratch_shapes=[pltpu.VMEM((Et, T), F32), pltpu.VMEM((Et, T), F32),
                        pltpu.VMEM((Et, T), BF16), pltpu.VMEM((d, T), F32)],
        compiler_params=_cparams(("arbitrary",)),
        name="peer_experts",
    )(xn, u, vt, s1p, s2, taup)


def peer_layer(h, norm_w, w_q, sub_keys, u_tab, v_tab):
    q, xn = rms_mm(h, norm_w, w_q.astype(BF16), BF16, emit_xn=True)
    s1p, s2, taup = peer_route(q, sub_keys.astype(BF16))
    return peer_experts(xn, u_tab.astype(BF16), v_tab.T.astype(BF16), s1p, s2, taup)


def _trunk(x, p, a_norm, a_in_proj, a_conv_w, a_conv_b, a_dt_bias, a_A_log, a_D, a_gnorm, a_out_proj,
           kv_norm, w_kvf, b_f, b_norm, b_wq, b_wo, c_norm, c_wq, c_subkeys, c_u, c_v,
           e_norm, e_wg, e_bg, e_wp, f_norm):
    depth = p.shape[0]
    n_a = a_norm.shape[0]
    h = x[0]
    shared = None
    for i in range(depth):
        if i < n_a:
            h = mamba_layer(h, a_norm[i], a_in_proj[i], a_conv_w[i], a_conv_b[i], a_dt_bias[i],
                            a_A_log[i], a_D[i], a_gnorm[i], a_out_proj[i])
        else:
            j = i - n_a
            h = fox_layer(h, b_norm[j], b_wq[j], b_wo[j], *shared)
        y = peer_layer(h, c_norm[i], c_wq[i], c_subkeys[i], c_u[i], c_v[i])
        h = ple_layer(h, y, e_norm[i], e_wg[i].astype(BF16), e_bg[i], p[i, 0].astype(BF16),
                      e_wp[i].astype(BF16))
        if i == n_a - 1:
            shared = fox_shared_kv(h, kv_norm, w_kvf, b_f)
    return final_norm(h, f_norm)[None]


def kernel(x, p, a_norm, a_in_proj, a_conv_w, a_conv_b, a_dt_bias, a_A_log, a_D, a_gnorm, a_out_proj, kv_norm, w_kvf, b_f, b_norm, b_wq, b_wo, c_norm, c_wq, c_subkeys, c_u, c_v, e_norm, e_wg, e_bg, e_wp, f_norm):
    assert x.shape[0] == 1
    return _trunk(x, p, a_norm, a_in_proj, a_conv_w, a_conv_b, a_dt_bias, a_A_log, a_D, a_gnorm,
                  a_out_proj, kv_norm, w_kvf, b_f, b_norm, b_wq, b_wo, c_norm, c_wq, c_subkeys,
                  c_u, c_v, e_norm, e_wg, e_bg, e_wp, f_norm)
```

```python
import functools
import math

import jax
import jax.numpy as jnp
from jax import lax
from jax.experimental import pallas as pl
from jax.experimental.pallas import tpu as pltpu

F32 = jnp.float32
BF16 = jnp.bfloat16
NEG_INF = float("-inf")

D_MODEL = 2048
RMS_EPS = 1e-6
GATED_NORM_EPS = 1e-5

D_INNER = 4096
SSM_HEAD_DIM = 64
SSM_HEADS = 64
SSM_GROUPS = 8
SSM_HEADS_PER_GROUP = SSM_HEADS // SSM_GROUPS
SSM_STATE = 128
D_CONV = 4
CONV_DIM = D_INNER + 2 * SSM_GROUPS * SSM_STATE
SSD_CHUNK = 256
GROUP_CH = D_INNER // SSM_GROUPS

FOX_HEADS = 16
FOX_HEAD_DIM = 128
ATT_DIM = FOX_HEADS * FOX_HEAD_DIM
FOX_SCALE = FOX_HEAD_DIM ** -0.5

PEER_HEADS = 8
PEER_N_KEYS = 128
PEER_EXPERTS = PEER_N_KEYS * PEER_N_KEYS
PEER_TOPK = 16
PEER_HALF = 128
PLE_DIM = 256

LANES = 128
SUBLANES = 8
VMEM_LIMIT = 56 * 1024 * 1024


def _cparams(sem):
    return pltpu.CompilerParams(dimension_semantics=sem, vmem_limit_bytes=VMEM_LIMIT)


def _rms(x, w, eps):
    return x * lax.rsqrt(jnp.mean(x * x, axis=-1, keepdims=True) + eps) * w


def _rms_mm_kernel(x_ref, nw_ref, w_ref, *rest, out_scale, emit_xn):
    if emit_xn:
        o_ref, xn_out_ref, xn_ref = rest
    else:
        o_ref, xn_ref = rest

    @pl.when(pl.program_id(1) == 0)
    def _():
        xn = _rms(x_ref[...], nw_ref[...], RMS_EPS).astype(BF16)
        xn_ref[...] = xn
        if emit_xn:
            xn_out_ref[...] = xn

    acc = jnp.dot(xn_ref[...], w_ref[...], preferred_element_type=F32)
    if out_scale != 1.0:
        acc = acc * out_scale
    o_ref[...] = acc.astype(o_ref.dtype)


def rms_mm(x, nw, w, out_dtype, *, tm=1024, tn=512, out_scale=1.0, emit_xn=False):
    s, d = x.shape
    n = w.shape[1]
    tn = min(tn, n)
    assert s % tm == 0 and n % tn == 0
    out_shape = [jax.ShapeDtypeStruct((s, n), out_dtype)]
    out_specs = [pl.BlockSpec((tm, tn), lambda i, j: (i, j))]
    if emit_xn:
        out_shape.append(jax.ShapeDtypeStruct((s, d), BF16))
        out_specs.append(pl.BlockSpec((tm, d), lambda i, j: (i, 0)))
    res = pl.pallas_call(
        functools.partial(_rms_mm_kernel, out_scale=out_scale, emit_xn=emit_xn),
        out_shape=out_shape,
        grid=(s // tm, n // tn),
        in_specs=[
            pl.BlockSpec((tm, d), lambda i, j: (i, 0)),
            pl.BlockSpec((1, d), lambda i, j: (0, 0)),
            pl.BlockSpec((d, tn), lambda i, j: (0, j)),
        ],
        out_specs=out_specs,
        scratch_shapes=[pltpu.VMEM((tm, d), BF16)],
        compiler_params=_cparams(("arbitrary", "arbitrary")),
        name="rms_mm",
    )(x, nw.reshape(1, d), w)
    return res if emit_xn else res[0]


def _mm_res_kernel(a_ref, w_ref, r_ref, o_ref):
    o_ref[...] = r_ref[...] + jnp.dot(a_ref[...], w_ref[...], preferred_element_type=F32)


def mm_res(a, w, res, *, tm=512, tn=512):
    s, k = a.shape
    n = w.shape[1]
    return pl.pallas_call(
        _mm_res_kernel,
        out_shape=jax.ShapeDtypeStruct((s, n), F32),
        grid=(s // tm, n // tn),
        in_specs=[
            pl.BlockSpec((tm, k), lambda i, j: (i, 0)),
            pl.BlockSpec((k, tn), lambda i, j: (0, j)),
            pl.BlockSpec((tm, tn), lambda i, j: (i, j)),
        ],
        out_specs=pl.BlockSpec((tm, tn), lambda i, j: (i, j)),
        compiler_params=_cparams(("arbitrary", "arbitrary")),
        name="mm_res",
    )(a, w, res)


def _ple_kernel(h_ref, y_ref, nw_ref, wg_ref, bg_ref, p_ref, wp_ref, o_ref, h1_ref, xn_ref, *, tn):
    j = pl.program_id(1)

    @pl.when(j == 0)
    def _():
        h1 = h_ref[...] + y_ref[...]
        h1_ref[...] = h1
        xn_ref[...] = _rms(h1, nw_ref[...], RMS_EPS).astype(BF16)

    gate = jax.nn.sigmoid(jnp.dot(xn_ref[...], wg_ref[...], preferred_element_type=F32) + bg_ref[...])
    proj = jnp.dot(p_ref[...], wp_ref[...], preferred_element_type=F32)
    col = pl.multiple_of(j * tn, tn)
    o_ref[...] = h1_ref[:, pl.ds(col, tn)] + gate * proj


def ple_layer(h, y, nw, wg, bg, p, wp, *, tm=512, tn=512):
    s, d = h.shape
    pd = p.shape[1]
    return pl.pallas_call(
        functools.partial(_ple_kernel, tn=tn),
        out_shape=jax.ShapeDtypeStruct((s, d), F32),
        grid=(s // tm, d // tn),
        in_specs=[
            pl.BlockSpec((tm, d), lambda i, j: (i, 0)),
            pl.BlockSpec((tm, d), lambda i, j: (i, 0)),
            pl.BlockSpec((1, d), lambda i, j: (0, 0)),
            pl.BlockSpec((d, tn), lambda i, j: (0, j)),
            pl.BlockSpec((1, tn), lambda i, j: (0, j)),
            pl.BlockSpec((tm, pd), lambda i, j: (i, 0)),
            pl.BlockSpec((pd, tn), lambda i, j: (0, j)),
        ],
        out_specs=pl.BlockSpec((tm, tn), lambda i, j: (i, j)),
        scratch_shapes=[pltpu.VMEM((tm, d), F32), pltpu.VMEM((tm, d), BF16)],
        compiler_params=_cparams(("arbitrary", "arbitrary")),
        name="ple",
    )(h, y, nw.reshape(1, d), wg, bg.reshape(1, d), p, wp)


def _final_norm_kernel(x_ref, w_ref, o_ref):
    o_ref[...] = _rms(x_ref[...], w_ref[...], RMS_EPS)


def final_norm(x, w, *, tm=512):
    s, d = x.shape
    return pl.pallas_call(
        _final_norm_kernel,
        out_shape=jax.ShapeDtypeStruct((s, d), F32),
        grid=(s // tm,),
        in_specs=[pl.BlockSpec((tm, d), lambda i: (i, 0)), pl.BlockSpec((1, d), lambda i: (0, 0))],
        out_specs=pl.BlockSpec((tm, d), lambda i: (i, 0)),
        compiler_params=_cparams(("arbitrary",)),
        name="final_norm",
    )(x, w.reshape(1, d))


def _tril_ones(n):
    r = lax.broadcasted_iota(jnp.int32, (n, n), 0)
    c = lax.broadcasted_iota(jnp.int32, (n, n), 1)
    return (c <= r).astype(F32)


N_SPLIT = 3
N_SCALES = 3


def _split3(v):
    hi = v.astype(BF16)
    r1 = v - hi.astype(F32)
    mid = r1.astype(BF16)
    lo = (r1 - mid.astype(F32)).astype(BF16)
    return hi, mid, lo


def _dt_prep_kernel(raw_ref, bias_ref, alog_ref, ac_ref, pk_ref):
    L = SSD_CHUNK
    w = raw_ref.shape[1]
    x = raw_ref[...] + bias_ref[...]
    dt = jnp.maximum(x, 0.0) + jnp.log1p(jnp.exp(-jnp.abs(x)))
    a = -jnp.exp(alog_ref[...])
    ac = jnp.dot(_tril_ones(L), dt * a, preferred_element_type=F32, precision=lax.Precision.HIGHEST)
    ac_ref[...] = ac
    scales = (dt, jnp.exp(ac), jnp.exp(ac[L - 1:L, :] - ac) * dt)
    for q, v in enumerate(scales):
        for k, piece in enumerate(_split3(v)):
            col = (q * N_SPLIT + k) * w
            pk_ref[:, col:col + w] = piece


def dt_prep(dt_raw, dt_bias, a_log):
    s, w = dt_raw.shape
    row = pl.BlockSpec((SSD_CHUNK, w), lambda c: (c, 0))
    vec = pl.BlockSpec((1, w), lambda c: (0, 0))
    wide = N_SCALES * N_SPLIT * w
    return pl.pallas_call(
        _dt_prep_kernel,
        out_shape=[jax.ShapeDtypeStruct((s, w), F32), jax.ShapeDtypeStruct((s, wide), BF16)],
        grid=(s // SSD_CHUNK,),
        in_specs=[row, vec, vec],
        out_specs=[row, pl.BlockSpec((SSD_CHUNK, wide), lambda c: (c, 0))],
        compiler_params=_cparams(("arbitrary",)),
        name="dt_prep",
    )(dt_raw, dt_bias, a_log)


def _silu(x):
    return x * jax.nn.sigmoid(x)


def _ssd_kernel(z_ref, x_ref, b_ref, c_ref, wx_ref, wb_ref, wc_ref, bx_ref, bb_ref, bc_ref,
                pk_ref, ac_ref, act_ref, dexp_ref, gw_ref, expand_ref, o_ref,
                xbuf, bbuf, cbuf, state_ref, yint_ref):
    L = SSD_CHUNK
    P = SSM_HEAD_DIM
    R = SSM_HEADS_PER_GROUP
    H = D_CONV - 1
    c_idx = pl.program_id(1)

    @pl.when(c_idx == 0)
    def _():
        xbuf[0:SUBLANES, :] = jnp.zeros((SUBLANES, GROUP_CH), F32)
        bbuf[0:SUBLANES, :] = jnp.zeros((SUBLANES, SSM_STATE), F32)
        cbuf[0:SUBLANES, :] = jnp.zeros((SUBLANES, SSM_STATE), F32)
        state_ref[...] = jnp.zeros_like(state_ref)

    def conv(buf, cur_ref, w_ref, bias_ref):
        cur = cur_ref[...].astype(F32)
        buf[SUBLANES:SUBLANES + L, :] = cur
        y = bias_ref[...] + w_ref[D_CONV - 1:D_CONV, :] * cur
        for k in range(H):
            y = y + w_ref[k:k + 1, :] * buf[pl.ds(SUBLANES - H + k, L), :]
        buf[0:SUBLANES, :] = cur[L - SUBLANES:L, :]
        return _silu(y)

    xg = conv(xbuf, x_ref, wx_ref, bx_ref)
    bm = conv(bbuf, b_ref, wb_ref, bb_ref)
    cm = conv(cbuf, c_ref, wc_ref, bc_ref)
    bb = bm.astype(BF16)
    cb16 = cm.astype(BF16)

    ac8 = ac_ref[...]
    act8 = act_ref[...]

    scales = jnp.dot(pk_ref[...], expand_ref[...], preferred_element_type=F32)
    dt_e = scales[:, 0:GROUP_CH]
    eac_e = scales[:, GROUP_CH:2 * GROUP_CH]
    w_e = scales[:, 2 * GROUP_CH:3 * GROUP_CH]
    elast_e = eac_e[L - 1:L, :]

    xdt = (xg * dt_e).astype(BF16)
    xw = (xg * w_e).astype(BF16)

    cbm = lax.dot_general(cb16, bb, (((1,), (1,)), ((), ())), preferred_element_type=F32)
    row = lax.broadcasted_iota(jnp.int32, (L, L), 0)
    col = lax.broadcasted_iota(jnp.int32, (L, L), 1)
    tril = col <= row
    for r in range(R):
        seg = ac8[:, r:r + 1] - act8[r:r + 1, :]
        decay = jnp.exp(jnp.where(tril, seg, NEG_INF))
        m = (cbm * decay).astype(BF16)
        yint_ref[:, r * P:(r + 1) * P] = jnp.dot(m, xdt[:, r * P:(r + 1) * P],
                                                  preferred_element_type=F32)

    state = state_ref[...]
    y_inter = jnp.dot(cb16, state.astype(BF16), preferred_element_type=F32) * eac_e
    bt = bm.T.astype(BF16)
    state_ref[...] = state * elast_e + jnp.dot(bt, xw, preferred_element_type=F32)

    y = yint_ref[...] + y_inter + dexp_ref[...] * xg
    gf = y * _silu(z_ref[...].astype(F32))
    o_ref[...] = _rms(gf, gw_ref[...], GATED_NORM_EPS).astype(o_ref.dtype)


def ssd_mixer(zx, conv_w, conv_b, pkg, acg, act, d_exp, gnorm_w):
    s = zx.shape[0]
    L = SSD_CHUNK
    G = SSM_GROUPS
    R = SSM_HEADS_PER_GROUP
    gn = SSM_STATE
    xoff = D_INNER // GROUP_CH
    boff = (2 * D_INNER) // gn
    coff = boff + G
    wboff = D_INNER // gn
    wcoff = wboff + G
    n_pk = N_SCALES * N_SPLIT * R
    pk_row = jnp.arange(n_pk)
    out_col = jnp.arange(N_SCALES * GROUP_CH)
    expand = ((pk_row[:, None] // (N_SPLIT * R) == out_col[None, :] // GROUP_CH)
              & (pk_row[:, None] % R == (out_col[None, :] % GROUP_CH) // SSM_HEAD_DIM)).astype(BF16)
    in_specs = [
        pl.BlockSpec((L, GROUP_CH), lambda g, c: (c, g)),
        pl.BlockSpec((L, GROUP_CH), lambda g, c: (c, xoff + g)),
        pl.BlockSpec((L, gn), lambda g, c: (c, boff + g)),
        pl.BlockSpec((L, gn), lambda g, c: (c, coff + g)),
        pl.BlockSpec((D_CONV, GROUP_CH), lambda g, c: (0, g)),
        pl.BlockSpec((D_CONV, gn), lambda g, c: (0, wboff + g)),
        pl.BlockSpec((D_CONV, gn), lambda g, c: (0, wcoff + g)),
        pl.BlockSpec((1, GROUP_CH), lambda g, c: (0, g)),
        pl.BlockSpec((1, gn), lambda g, c: (0, wboff + g)),
        pl.BlockSpec((1, gn), lambda g, c: (0, wcoff + g)),
        pl.BlockSpec((None, L, n_pk), lambda g, c: (g, c, 0)),
        pl.BlockSpec((None, L, R), lambda g, c: (g, c, 0)),
        pl.BlockSpec((None, R, L), lambda g, c: (g, 0, c)),
        pl.BlockSpec((1, GROUP_CH), lambda g, c: (0, g)),
        pl.BlockSpec((1, GROUP_CH), lambda g, c: (0, g)),
        pl.BlockSpec((n_pk, N_SCALES * GROUP_CH), lambda g, c: (0, 0)),
    ]
    return pl.pallas_call(
        _ssd_kernel,
        out_shape=jax.ShapeDtypeStruct((s, D_INNER), BF16),
        grid=(G, s // L),
        in_specs=in_specs,
        out_specs=pl.BlockSpec((L, GROUP_CH), lambda g, c: (c, g)),
        scratch_shapes=[
            pltpu.VMEM((L + SUBLANES, GROUP_CH), F32),
            pltpu.VMEM((L + SUBLANES, gn), F32),
            pltpu.VMEM((L + SUBLANES, gn), F32),
            pltpu.VMEM((gn, GROUP_CH), F32),
            pltpu.VMEM((L, GROUP_CH), F32),
        ],
        compiler_params=_cparams(("arbitrary", "arbitrary")),
        name="ssd",
    )(zx, zx, zx, zx, conv_w, conv_w, conv_w, conv_b, conv_b, conv_b,
      pkg, acg, act, d_exp, gnorm_w, expand)


def mamba_layer(h, norm_w, w_in, conv_w, conv_b, dt_bias, a_log, d_skip, gnorm_w, w_out):
    s = h.shape[0]
    n_zx = D_INNER + CONV_DIM
    w_zx = w_in[:, :n_zx].astype(BF16)
    w_dt = jnp.pad(w_in[:, n_zx:], ((0, 0), (0, LANES - SSM_HEADS))).astype(BF16)
    zx = rms_mm(h, norm_w, w_zx, BF16)
    dt_raw = rms_mm(h, norm_w, w_dt, F32)
    pad = lambda v: jnp.pad(v.astype(F32), (0, LANES - SSM_HEADS)).reshape(1, LANES)
    ac, pk = dt_prep(dt_raw, pad(dt_bias), pad(a_log))
    acg = ac[:, :SSM_HEADS].reshape(s, SSM_GROUPS, SSM_HEADS_PER_GROUP).transpose(1, 0, 2)
    act = acg.transpose(0, 2, 1)
    pkg = pk.reshape(s, N_SCALES * N_SPLIT, LANES)[:, :, :SSM_HEADS]
    pkg = pkg.reshape(s, N_SCALES * N_SPLIT, SSM_GROUPS, SSM_HEADS_PER_GROUP).transpose(2, 0, 1, 3)
    pkg = pkg.reshape(SSM_GROUPS, s, N_SCALES * N_SPLIT * SSM_HEADS_PER_GROUP)
    d_exp = jnp.repeat(d_skip.astype(F32), SSM_HEAD_DIM).reshape(1, D_INNER)
    y = ssd_mixer(zx, conv_w, conv_b.reshape(1, CONV_DIM), pkg, acg, act, d_exp,
                  gnorm_w.reshape(1, D_INNER))
    return mm_res(y, w_out.astype(BF16), h)


CUM_TILE = 256


def _cum_kernel(f_ref, b_ref, o_ref, carry_ref):
    @pl.when(pl.program_id(0) == 0)
    def _():
        carry_ref[...] = jnp.zeros_like(carry_ref)

    log_f = jax.nn.log_sigmoid(f_ref[...] + b_ref[...])
    cum = jnp.dot(_tril_ones(CUM_TILE), log_f, preferred_element_type=F32,
                  precision=lax.Precision.HIGHEST) + carry_ref[...]
    o_ref[...] = cum
    carry_ref[...] = cum[CUM_TILE - 1:CUM_TILE, :]


def fox_cum(f_raw, b_f):
    s, w = f_raw.shape
    return pl.pallas_call(
        _cum_kernel,
        out_shape=jax.ShapeDtypeStruct((s, w), F32),
        grid=(s // CUM_TILE,),
        in_specs=[pl.BlockSpec((CUM_TILE, w), lambda i: (i, 0)), pl.BlockSpec((1, w), lambda i: (0, 0))],
        out_specs=pl.BlockSpec((CUM_TILE, w), lambda i: (i, 0)),
        scratch_shapes=[pltpu.VMEM((1, w), F32)],
        compiler_params=_cparams(("arbitrary",)),
        name="fox_cum",
    )(f_raw, b_f)


FOX_TILE = 512


FOX_AUG = 2 * FOX_HEAD_DIM
LOG2E = 1.4426950408889634


def _fox_kernel(q_ref, k_ref, v_ref, cq_ref, ck_ref, o_ref, kaug, vt, qaug, za, zb, m_ref, l_ref, acc_ref):
    i = pl.program_id(1)
    T = FOX_TILE
    D = FOX_HEAD_DIM
    lane = lax.broadcasted_iota(jnp.int32, (1, D), 1)

    @pl.when(i == 0)
    def _():
        c2 = ck_ref[...] * (-LOG2E)
        hi = c2.astype(BF16)
        r1 = c2 - hi.astype(F32)
        mid = r1.astype(BF16)
        lo = (r1 - mid.astype(F32)).astype(BF16)
        aug = jnp.where(lane == 0, hi.astype(F32),
                        jnp.where(lane == 1, mid.astype(F32),
                                  jnp.where(lane == 2, lo.astype(F32), 0.0)))
        kaug[:, 0:D] = k_ref[...]
        kaug[:, D:FOX_AUG] = aug.astype(BF16)
        vt[...] = v_ref[...].astype(F32).T.astype(BF16)

    qaug[:, 0:D] = q_ref[...]
    qaug[:, D:FOX_AUG] = jnp.broadcast_to(jnp.where(lane < 3, 1.0, 0.0), (T, D)).astype(BF16)
    m_ref[...] = jnp.full_like(m_ref, NEG_INF)
    l_ref[...] = jnp.zeros_like(l_ref)
    acc_ref[...] = jnp.zeros_like(acc_ref)
    cq2 = cq_ref[...] * LOG2E
    nt = (((1,), (1,)), ((), ()))

    def logits(j, z_ref):
        off = pl.multiple_of(j * T, T)
        z_ref[...] = lax.dot_general(kaug[pl.ds(off, T), :], qaug[...], nt,
                                     preferred_element_type=F32)

    def softmax_pv(j, z_ref, masked):
        off = pl.multiple_of(j * T, T)
        z = z_ref[...]
        if masked:
            key = lax.broadcasted_iota(jnp.int32, (T, T), 0)
            qry = lax.broadcasted_iota(jnp.int32, (T, T), 1)
            z = jnp.where(key <= qry, z, NEG_INF)
        m_old = m_ref[...]
        m_new = jnp.maximum(m_old, jnp.max(z, axis=0, keepdims=True) + cq2)
        p = jnp.exp2(z + (cq2 - m_new))
        alpha = jnp.exp2(m_old - m_new)
        l_ref[...] = alpha * l_ref[...] + jnp.sum(p, axis=0, keepdims=True)
        acc_ref[...] = alpha * acc_ref[...] + jnp.dot(vt[:, pl.ds(off, T)], p.astype(BF16),
                                                      preferred_element_type=F32)
        m_ref[...] = m_new

    logits(0, za)

    def pair(pp, carry):
        j = 2 * pp
        logits(j + 1, zb)
        softmax_pv(j, za, False)
        logits(j + 2, za)
        softmax_pv(j + 1, zb, False)
        return carry

    lax.fori_loop(0, i // 2, pair, 0)

    @pl.when(i % 2 == 0)
    def _():
        softmax_pv(i, za, True)

    @pl.when(i % 2 == 1)
    def _():
        logits(i, zb)
        softmax_pv(i - 1, za, False)
        softmax_pv(i, zb, True)

    o_ref[...] = (acc_ref[...] / l_ref[...]).T.astype(o_ref.dtype)


def fox_attention(q, kv, cum):
    s = q.shape[0]
    T = FOX_TILE
    D = FOX_HEAD_DIM
    cq = cum[:, :FOX_HEADS].T.reshape(FOX_HEADS, 1, s)
    ck = cum[:, :FOX_HEADS].T.reshape(FOX_HEADS, s, 1)
    return pl.pallas_call(
        _fox_kernel,
        out_shape=jax.ShapeDtypeStruct((s, ATT_DIM), BF16),
        grid=(FOX_HEADS, s // T),
        in_specs=[
            pl.BlockSpec((T, D), lambda h, i: (i, h)),
            pl.BlockSpec((s, D), lambda h, i: (0, h)),
            pl.BlockSpec((s, D), lambda h, i: (0, FOX_HEADS + h)),
            pl.BlockSpec((None, 1, T), lambda h, i: (h, 0, i)),
            pl.BlockSpec((None, s, 1), lambda h, i: (h, 0, 0)),
        ],
        out_specs=pl.BlockSpec((T, D), lambda h, i: (i, h)),
        scratch_shapes=[pltpu.VMEM((s, FOX_AUG), BF16), pltpu.VMEM((D, s), BF16),
                        pltpu.VMEM((T, FOX_AUG), BF16),
                        pltpu.VMEM((T, T), F32), pltpu.VMEM((T, T), F32),
                        pltpu.VMEM((1, T), F32), pltpu.VMEM((1, T), F32), pltpu.VMEM((D, T), F32)],
        compiler_params=_cparams(("arbitrary", "arbitrary")),
        name="fox_attn",
    )(q, kv, kv, cq, ck)


def fox_shared_kv(h, kv_norm, w_kvf, b_f):
    w_kv = w_kvf[:, :2 * ATT_DIM].astype(BF16)
    w_f = jnp.pad(w_kvf[:, 2 * ATT_DIM:], ((0, 0), (0, LANES - FOX_HEADS))).astype(BF16)
    kv = rms_mm(h, kv_norm, w_kv, BF16)
    f_raw = rms_mm(h, kv_norm, w_f, F32)
    b_pad = jnp.pad(b_f.astype(F32), (0, LANES - FOX_HEADS)).reshape(1, LANES)
    return kv, fox_cum(f_raw, b_pad)


def fox_layer(h, norm_w, w_q, w_o, kv, cum):
    q = rms_mm(h, norm_w, w_q.astype(BF16), BF16, out_scale=FOX_SCALE * LOG2E)
    o = fox_attention(q, kv, cum)
    return mm_res(o, w_o.astype(BF16), h)


ROUTE_TILE = 256
N_TOP = PEER_TOPK + 1
TOP_ROWS = 24
_CAND_NB = [N_TOP // (a + 1) for a in range(1, SUBLANES)]
CAND_ROWS = TOP_ROWS + SUBLANES * (SUBLANES - 1) + (TOP_ROWS - SUBLANES)


def _extract_top(x, out_ref, n):
    for k in range(n):
        m = jnp.max(x, axis=0, keepdims=True)
        out_ref[k:k + 1, :] = m
        x = jnp.where(x >= m, NEG_INF, x)


def _route_kernel(q_ref, keys_ref, s1p_ref, s2_ref, taup_ref, v1_ref, v2_ref, ct_ref):
    T = ROUTE_TILE

    def head(h, carry):
        v1_ref[...] = jnp.full_like(v1_ref, NEG_INF)
        v2_ref[...] = jnp.full_like(v2_ref, NEG_INF)
        ct_ref[...] = jnp.full_like(ct_ref, NEG_INF)
        col = pl.multiple_of(h * 2 * PEER_HALF, 2 * PEER_HALF)
        q1 = q_ref[:, pl.ds(col, PEER_HALF)]
        q2 = q_ref[:, pl.ds(col + PEER_HALF, PEER_HALF)]
        nt = (((1,), (1,)), ((), ()))
        s1 = lax.dot_general(keys_ref[h, 0], q1, nt, preferred_element_type=F32)
        s2 = lax.dot_general(keys_ref[h, 1], q2, nt, preferred_element_type=F32)
        _extract_top(s1, v1_ref, N_TOP)
        _extract_top(s2, v2_ref, N_TOP)
        v1 = v1_ref[...]
        v2 = v2_ref[...]
        pieces = [v1[0:1, :] + v2]
        rows8 = lax.broadcasted_iota(jnp.int32, (SUBLANES, T), 0)
        for a in range(1, SUBLANES):
            pieces.append(v1[a:a + 1, :] + jnp.where(rows8 < _CAND_NB[a - 1], v2[0:SUBLANES, :], NEG_INF))
        pieces.append(v1[SUBLANES:TOP_ROWS, :] + v2[0:1, :])
        cand = jnp.concatenate(pieces, axis=0)
        _extract_top(cand, ct_ref, N_TOP)
        m = ct_ref[0:1, :]
        tau = 0.5 * (ct_ref[PEER_TOPK - 1:PEER_TOPK, :] + ct_ref[PEER_TOPK:PEER_TOPK + 1, :])
        zsum = jnp.sum(jnp.where(cand >= tau, jnp.exp(cand - m), 0.0), axis=0, keepdims=True)
        shift = m + jnp.log(zsum)
        s1p_ref[h] = (s1 - shift) * LOG2E
        s2_ref[h] = s2 * LOG2E
        taup_ref[pl.ds(h, 1), :] = (tau - shift) * LOG2E
        return carry

    lax.fori_loop(0, PEER_HEADS, head, 0)


def peer_route(q, keys):
    s = q.shape[0]
    T = ROUTE_TILE
    big = pl.BlockSpec((PEER_HEADS, PEER_N_KEYS, T), lambda t: (0, 0, t))
    return pl.pallas_call(
        _route_kernel,
        out_shape=[jax.ShapeDtypeStruct((PEER_HEADS, PEER_N_KEYS, s), F32)] * 2
        + [jax.ShapeDtypeStruct((PEER_HEADS, s), F32)],
        grid=(s // T,),
        in_specs=[
            pl.BlockSpec((T, PEER_HEADS * 2 * PEER_HALF), lambda t: (t, 0)),
            pl.BlockSpec((PEER_HEADS, 2, PEER_N_KEYS, PEER_HALF), lambda t: (0, 0, 0, 0)),
        ],
        out_specs=[big, big, pl.BlockSpec((PEER_HEADS, T), lambda t: (0, t))],
        scratch_shapes=[pltpu.VMEM((TOP_ROWS, T), F32), pltpu.VMEM((TOP_ROWS, T), F32),
                        pltpu.VMEM((TOP_ROWS, T), F32)],
        compiler_params=_cparams(("arbitrary",)),
        name="peer_route",
    )(q, keys)


PEER_TOK_TILE = 512
PEER_EXP_TILE = 512
PEER_EXP_SUB = 512
PEER_TOK_GROUP = 256
PEER_ACT_PARTS = 1
_SQRT_HALF = 0.7071067811865476


def _gelu(x):
    return 0.5 * x * (1.0 + lax.erf(x * _SQRT_HALF))


def _peer_kernel(xn_ref, u_ref, vt_ref, s1p_ref, s2_ref, taup_ref, o_ref,
                 at0, at1, ga, acc_ref, *, n_e):
    s = pl.program_id(0)
    T = PEER_TOK_TILE
    n_i = PEER_EXP_TILE // PEER_N_KEYS
    n_items = pl.num_programs(0) - 1
    e_out = jnp.clip(s - 1, 0, n_items - 1) % n_e

    @pl.when(s == 0)
    def _():
        at1[...] = jnp.zeros_like(at1)

    @pl.when(e_out == 0)
    def _():
        acc_ref[...] = jnp.zeros_like(acc_ref)

    HT = PEER_TOK_GROUP

    def gate_chunk(at_r, ga_w, lane0, ii, key0):
        lanes = pl.ds(lane0, LANES)
        rb = 2 * SUBLANES
        k1 = key0 + ii
        s1b = [jnp.broadcast_to(s1p_ref[h, k1:k1 + 1, lanes], (SUBLANES, LANES)) for h in range(PEER_HEADS)]
        tb = [jnp.broadcast_to(taup_ref[h:h + 1, lanes], (SUBLANES, LANES)) for h in range(PEER_HEADS)]
        for blk in range(PEER_N_KEYS // rb):
            parts = []
            for sub in range(rb // SUBLANES):
                keys = slice(blk * rb + sub * SUBLANES, blk * rb + (sub + 1) * SUBLANES)
                g = jnp.zeros((SUBLANES, LANES), F32)
                for h in range(PEER_HEADS):
                    ssum = s1b[h] + s2_ref[h, keys, lanes]
                    g = g + jnp.where(ssum >= tb[h], jnp.exp2(ssum), 0.0)
                parts.append(g)
            rows = slice(ii * PEER_N_KEYS + blk * rb, ii * PEER_N_KEYS + (blk + 1) * rb)
            g16 = jnp.concatenate(parts, axis=0)
            ga_w[rows, lanes] = (g16 * _gelu(at_r[rows, lanes])).astype(BF16)

    def step(at_w, at_r, key0):
        n_groups = T // HT
        act_cols = T // PEER_ACT_PARTS
        n_sub = PEER_EXP_SUB // PEER_N_KEYS
        for it in range(PEER_EXP_TILE // PEER_EXP_SUB):
            erows = slice(it * PEER_EXP_SUB, (it + 1) * PEER_EXP_SUB)
            for grp in range(n_groups):
                cols = slice(grp * HT, (grp + 1) * HT)
                for sub in range(HT // LANES):
                    for ii in range(n_sub):
                        gate_chunk(at_r, ga, grp * HT + sub * LANES, it * n_sub + ii, key0)
                acc_ref[:, cols] += jnp.dot(vt_ref[:, erows], ga[erows, cols], preferred_element_type=F32)
                if grp < PEER_ACT_PARTS:
                    acols = slice(grp * act_cols, (grp + 1) * act_cols)
                    at_w[erows, acols] = lax.dot_general(u_ref[erows, :], xn_ref[acols, :],
                                                         (((1,), (1,)), ((), ())), preferred_element_type=F32)

    @pl.when(s % 2 == 0)
    def _():
        step(at0, at1, n_i)

    @pl.when(s % 2 == 1)
    def _():
        step(at1, at0, 0)

    @pl.when((e_out == n_e - 1) & (s >= 1))
    def _():
        o_ref[...] = acc_ref[...].T


def peer_experts(xn, u, vt, s1p, s2, taup):
    s, d = xn.shape
    T = PEER_TOK_TILE
    Et = PEER_EXP_TILE
    n_i = Et // PEER_N_KEYS
    n_e = PEER_EXPERTS // Et
    n_items = (s // T) * n_e
    assert n_e % 2 == 0 and 2 * n_i == SUBLANES

    def item(step, lag):
        return jnp.clip(step - lag, 0, n_items - 1)

    tok = lambda step, lag: item(step, lag) // n_e
    exp = lambda step, lag: item(step, lag) % n_e
    return pl.pallas_call(
        functools.partial(_peer_kernel, n_e=n_e),
        out_shape=jax.ShapeDtypeStruct((s, d), F32),
        grid=(n_items + 1,),
        in_specs=[
            pl.BlockSpec((T, d), lambda i: (tok(i, 0), 0)),
            pl.BlockSpec((Et, d), lambda i: (exp(i, 0), 0)),
            pl.BlockSpec((d, Et), lambda i: (0, exp(i, 1))),
            pl.BlockSpec((PEER_HEADS, SUBLANES, T), lambda i: (0, exp(i, 1) // 2, tok(i, 1))),
            pl.BlockSpec((PEER_HEADS, PEER_N_KEYS, T), lambda i: (0, 0, tok(i, 1))),
            pl.BlockSpec((PEER_HEADS, T), lambda i: (0, tok(i, 1))),
        ],
        out_specs=pl.BlockSpec((T, d), lambda i: (tok(i, 1), 0)),
        scratch_shapes=[pltpu.VMEM((Et, T), F32), pltpu.VMEM((Et, T), F32),
                        pltpu.VMEM((Et, T), BF16), pltpu.VMEM((d, T), F32)],
        compiler_params=_cparams(("arbitrary",)),
        name="peer_experts",
    )(xn, u, vt, s1p, s2, taup)


def _tables_kernel(u_ref, v_ref, ub_ref, vt_ref):
    ub_ref[...] = u_ref[...].astype(BF16)
    vt_ref[...] = v_ref[...].T.astype(BF16)


def peer_tables(u_all, v_all, layer, *, te=512):
    _, n, d = u_all.shape
    tab = pl.BlockSpec((None, te, d), lambda i: (layer, i, 0))
    return pl.pallas_call(
        _tables_kernel,
        out_shape=[jax.ShapeDtypeStruct((n, d), BF16), jax.ShapeDtypeStruct((d, n), BF16)],
        grid=(n // te,),
        in_specs=[tab, tab],
        out_specs=[pl.BlockSpec((te, d), lambda i: (i, 0)), pl.BlockSpec((d, te), lambda i: (0, i))],
        compiler_params=_cparams(("arbitrary",)),
        name="peer_tables",
    )(u_all, v_all)


def peer_layer(h, norm_w, w_q, sub_keys, u_all, v_all, layer):
    q, xn = rms_mm(h, norm_w, w_q.astype(BF16), BF16, emit_xn=True)
    s1p, s2, taup = peer_route(q, sub_keys.astype(BF16))
    ub, vt = peer_tables(u_all, v_all, layer)
    return peer_experts(xn, ub, vt, s1p, s2, taup)


def _trunk(x, p, a_norm, a_in_proj, a_conv_w, a_conv_b, a_dt_bias, a_A_log, a_D, a_gnorm, a_out_proj,
           kv_norm, w_kvf, b_f, b_norm, b_wq, b_wo, c_norm, c_wq, c_subkeys, c_u, c_v,
           e_norm, e_wg, e_bg, e_wp, f_norm):
    depth = p.shape[0]
    n_a = a_norm.shape[0]
    h = x[0]
    shared = None
    for i in range(depth):
        if i < n_a:
            h = mamba_layer(h, a_norm[i], a_in_proj[i], a_conv_w[i], a_conv_b[i], a_dt_bias[i],
                            a_A_log[i], a_D[i], a_gnorm[i], a_out_proj[i])
        else:
            j = i - n_a
            h = fox_layer(h, b_norm[j], b_wq[j], b_wo[j], *shared)
        y = peer_layer(h, c_norm[i], c_wq[i], c_subkeys[i], c_u, c_v, i)
        h = ple_layer(h, y, e_norm[i], e_wg[i].astype(BF16), e_bg[i], p[i, 0].astype(BF16),
                      e_wp[i].astype(BF16))
        if i == n_a - 1:
            shared = fox_shared_kv(h, kv_norm, w_kvf, b_f)
    return final_norm(h, f_norm)[None]


def kernel(x, p, a_norm, a_in_proj, a_conv_w, a_conv_b, a_dt_bias, a_A_log, a_D, a_gnorm, a_out_proj, kv_norm, w_kvf, b_f, b_norm, b_wq, b_wo, c_norm, c_wq, c_subkeys, c_u, c_v, e_norm, e_wg, e_bg, e_wp, f_norm):
    assert x.shape[0] == 1
    return _trunk(x, p, a_norm, a_in_proj, a_conv_w, a_conv_b, a_dt_bias, a_A_log, a_D, a_gnorm,
                  a_out_proj, kv_norm, w_kvf, b_f, b_norm, b_wq, b_wo, c_norm, c_wq, c_subkeys,
                  c_u, c_v, e_norm, e_wg, e_bg, e_wp, f_norm)
```

```python
import functools
import math

import jax
import jax.numpy as jnp
from jax import lax
from jax.experimental import pallas as pl
from jax.experimental.pallas import tpu as pltpu

F32 = jnp.float32
BF16 = jnp.bfloat16
NEG_INF = float("-inf")
LOG2E = 1.4426950408889634

D_MODEL = 2048
RMS_EPS = 1e-6
GATED_NORM_EPS = 1e-5

D_INNER = 4096
SSM_HEAD_DIM = 64
SSM_HEADS = 64
SSM_GROUPS = 8
SSM_HEADS_PER_GROUP = SSM_HEADS // SSM_GROUPS
SSM_STATE = 128
D_CONV = 4
CONV_DIM = D_INNER + 2 * SSM_GROUPS * SSM_STATE
SSD_CHUNK = 256
GROUP_CH = D_INNER // SSM_GROUPS

FOX_HEADS = 16
FOX_HEAD_DIM = 128
ATT_DIM = FOX_HEADS * FOX_HEAD_DIM
FOX_SCALE = FOX_HEAD_DIM ** -0.5

PEER_HEADS = 8
PEER_N_KEYS = 128
PEER_EXPERTS = PEER_N_KEYS * PEER_N_KEYS
PEER_TOPK = 16
PEER_HALF = 128
PLE_DIM = 256

LANES = 128
SUBLANES = 8
VMEM_LIMIT = 56 * 1024 * 1024


def _cparams(sem):
    return pltpu.CompilerParams(dimension_semantics=sem, vmem_limit_bytes=VMEM_LIMIT)


def _rms(x, w, eps):
    return x * lax.rsqrt(jnp.mean(x * x, axis=-1, keepdims=True) + eps) * w


def _rms_mm_kernel(x_ref, nw_ref, w_ref, *rest, out_scale, emit_xn):
    if emit_xn:
        o_ref, xn_out_ref, xn_ref = rest
    else:
        o_ref, xn_ref = rest

    @pl.when(pl.program_id(1) == 0)
    def _():
        xn = _rms(x_ref[...], nw_ref[...], RMS_EPS).astype(BF16)
        xn_ref[...] = xn
        if emit_xn:
            xn_out_ref[...] = xn

    acc = jnp.dot(xn_ref[...], w_ref[...], preferred_element_type=F32)
    if out_scale != 1.0:
        acc = acc * out_scale
    o_ref[...] = acc.astype(o_ref.dtype)


def rms_mm(x, nw, w, out_dtype, *, tm=1024, tn=512, out_scale=1.0, emit_xn=False):
    s, d = x.shape
    n = w.shape[1]
    tn = min(tn, n)
    assert s % tm == 0 and n % tn == 0
    out_shape = [jax.ShapeDtypeStruct((s, n), out_dtype)]
    out_specs = [pl.BlockSpec((tm, tn), lambda i, j: (i, j))]
    if emit_xn:
        out_shape.append(jax.ShapeDtypeStruct((s, d), BF16))
        out_specs.append(pl.BlockSpec((tm, d), lambda i, j: (i, 0)))
    res = pl.pallas_call(
        functools.partial(_rms_mm_kernel, out_scale=out_scale, emit_xn=emit_xn),
        out_shape=out_shape,
        grid=(s // tm, n // tn),
        in_specs=[
            pl.BlockSpec((tm, d), lambda i, j: (i, 0)),
            pl.BlockSpec((1, d), lambda i, j: (0, 0)),
            pl.BlockSpec((d, tn), lambda i, j: (0, j)),
        ],
        out_specs=out_specs,
        scratch_shapes=[pltpu.VMEM((tm, d), BF16)],
        compiler_params=_cparams(("arbitrary", "arbitrary")),
        name="rms_mm",
    )(x, nw.reshape(1, d), w)
    return res if emit_xn else res[0]


def _mm_res_kernel(a_ref, w_ref, r_ref, o_ref):
    o_ref[...] = r_ref[...] + jnp.dot(a_ref[...], w_ref[...], preferred_element_type=F32)


def mm_res(a, w, res, *, tm=512, tn=512):
    s, k = a.shape
    n = w.shape[1]
    return pl.pallas_call(
        _mm_res_kernel,
        out_shape=jax.ShapeDtypeStruct((s, n), F32),
        grid=(s // tm, n // tn),
        in_specs=[
            pl.BlockSpec((tm, k), lambda i, j: (i, 0)),
            pl.BlockSpec((k, tn), lambda i, j: (0, j)),
            pl.BlockSpec((tm, tn), lambda i, j: (i, j)),
        ],
        out_specs=pl.BlockSpec((tm, tn), lambda i, j: (i, j)),
        compiler_params=_cparams(("arbitrary", "arbitrary")),
        name="mm_res",
    )(a, w, res)


def _ple_kernel(h_ref, y_ref, nw_ref, wg_ref, bg_ref, p_ref, wp_ref, o_ref, h1_ref, xn_ref, *, tn):
    j = pl.program_id(1)

    @pl.when(j == 0)
    def _():
        h1 = h_ref[...] + y_ref[...]
        h1_ref[...] = h1
        xn_ref[...] = _rms(h1, nw_ref[...], RMS_EPS).astype(BF16)

    gate = jax.nn.sigmoid(jnp.dot(xn_ref[...], wg_ref[...], preferred_element_type=F32) + bg_ref[...])
    proj = jnp.dot(p_ref[...], wp_ref[...], preferred_element_type=F32)
    col = pl.multiple_of(j * tn, tn)
    o_ref[...] = h1_ref[:, pl.ds(col, tn)] + gate * proj


def ple_layer(h, y, nw, wg, bg, p, wp, *, tm=512, tn=512):
    s, d = h.shape
    pd = p.shape[1]
    return pl.pallas_call(
        functools.partial(_ple_kernel, tn=tn),
        out_shape=jax.ShapeDtypeStruct((s, d), F32),
        grid=(s // tm, d // tn),
        in_specs=[
            pl.BlockSpec((tm, d), lambda i, j: (i, 0)),
            pl.BlockSpec((tm, d), lambda i, j: (i, 0)),
            pl.BlockSpec((1, d), lambda i, j: (0, 0)),
            pl.BlockSpec((d, tn), lambda i, j: (0, j)),
            pl.BlockSpec((1, tn), lambda i, j: (0, j)),
            pl.BlockSpec((tm, pd), lambda i, j: (i, 0)),
            pl.BlockSpec((pd, tn), lambda i, j: (0, j)),
        ],
        out_specs=pl.BlockSpec((tm, tn), lambda i, j: (i, j)),
        scratch_shapes=[pltpu.VMEM((tm, d), F32), pltpu.VMEM((tm, d), BF16)],
        compiler_params=_cparams(("arbitrary", "arbitrary")),
        name="ple",
    )(h, y, nw.reshape(1, d), wg, bg.reshape(1, d), p, wp)


def _final_norm_kernel(x_ref, w_ref, o_ref):
    o_ref[...] = _rms(x_ref[...], w_ref[...], RMS_EPS)


def final_norm(x, w, *, tm=512):
    s, d = x.shape
    return pl.pallas_call(
        _final_norm_kernel,
        out_shape=jax.ShapeDtypeStruct((s, d), F32),
        grid=(s // tm,),
        in_specs=[pl.BlockSpec((tm, d), lambda i: (i, 0)), pl.BlockSpec((1, d), lambda i: (0, 0))],
        out_specs=pl.BlockSpec((tm, d), lambda i: (i, 0)),
        compiler_params=_cparams(("arbitrary",)),
        name="final_norm",
    )(x, w.reshape(1, d))


def _tril_ones(n):
    r = lax.broadcasted_iota(jnp.int32, (n, n), 0)
    c = lax.broadcasted_iota(jnp.int32, (n, n), 1)
    return (c <= r).astype(F32)


N_SPLIT = 3
N_SCALES = 3


def _split3(v):
    hi = v.astype(BF16)
    r1 = v - hi.astype(F32)
    mid = r1.astype(BF16)
    lo = (r1 - mid.astype(F32)).astype(BF16)
    return hi, mid, lo


def _dt_prep_kernel(raw_ref, bias_ref, alog_ref, ac_ref, pk_ref):
    L = SSD_CHUNK
    w = raw_ref.shape[1]
    x = raw_ref[...] + bias_ref[...]
    dt = jnp.maximum(x, 0.0) + jnp.log1p(jnp.exp(-jnp.abs(x)))
    a = -jnp.exp(alog_ref[...])
    ac = jnp.dot(_tril_ones(L), dt * a, preferred_element_type=F32, precision=lax.Precision.HIGHEST)
    ac_ref[...] = ac * LOG2E
    scales = (dt, jnp.exp(ac), jnp.exp(ac[L - 1:L, :] - ac) * dt)
    for q, v in enumerate(scales):
        for k, piece in enumerate(_split3(v)):
            col = (q * N_SPLIT + k) * w
            pk_ref[:, col:col + w] = piece


def dt_prep(dt_raw, dt_bias, a_log):
    s, w = dt_raw.shape
    row = pl.BlockSpec((SSD_CHUNK, w), lambda c: (c, 0))
    vec = pl.BlockSpec((1, w), lambda c: (0, 0))
    wide = N_SCALES * N_SPLIT * w
    return pl.pallas_call(
        _dt_prep_kernel,
        out_shape=[jax.ShapeDtypeStruct((s, w), F32), jax.ShapeDtypeStruct((s, wide), BF16)],
        grid=(s // SSD_CHUNK,),
        in_specs=[row, vec, vec],
        out_specs=[row, pl.BlockSpec((SSD_CHUNK, wide), lambda c: (c, 0))],
        compiler_params=_cparams(("arbitrary",)),
        name="dt_prep",
    )(dt_raw, dt_bias, a_log)


def _silu(x):
    return x * jax.nn.sigmoid(x)


def _ssd_kernel(z_ref, x_ref, b_ref, c_ref, wx_ref, wb_ref, wc_ref, bx_ref, bb_ref, bc_ref,
                pk_ref, ac_ref, act_ref, dexp_ref, gw_ref, expand_ref, o_ref,
                xbuf, bbuf, cbuf, state_ref, yint_ref):
    L = SSD_CHUNK
    P = SSM_HEAD_DIM
    R = SSM_HEADS_PER_GROUP
    H = D_CONV - 1
    c_idx = pl.program_id(1)

    @pl.when(c_idx == 0)
    def _():
        xbuf[0:SUBLANES, :] = jnp.zeros((SUBLANES, GROUP_CH), F32)
        bbuf[0:SUBLANES, :] = jnp.zeros((SUBLANES, SSM_STATE), F32)
        cbuf[0:SUBLANES, :] = jnp.zeros((SUBLANES, SSM_STATE), F32)
        state_ref[...] = jnp.zeros_like(state_ref)

    def conv(buf, cur_ref, w_ref, bias_ref):
        cur = cur_ref[...].astype(F32)
        buf[SUBLANES:SUBLANES + L, :] = cur
        y = bias_ref[...] + w_ref[D_CONV - 1:D_CONV, :] * cur
        for k in range(H):
            y = y + w_ref[k:k + 1, :] * buf[pl.ds(SUBLANES - H + k, L), :]
        buf[0:SUBLANES, :] = cur[L - SUBLANES:L, :]
        return _silu(y)

    xg = conv(xbuf, x_ref, wx_ref, bx_ref)
    bm = conv(bbuf, b_ref, wb_ref, bb_ref)
    cm = conv(cbuf, c_ref, wc_ref, bc_ref)
    bb = bm.astype(BF16)
    cb16 = cm.astype(BF16)

    ac8 = ac_ref[...]
    act8 = act_ref[...]

    scales = jnp.dot(pk_ref[...], expand_ref[...], preferred_element_type=F32)
    dt_e = scales[:, 0:GROUP_CH]
    eac_e = scales[:, GROUP_CH:2 * GROUP_CH]
    w_e = scales[:, 2 * GROUP_CH:3 * GROUP_CH]
    elast_e = eac_e[L - 1:L, :]

    xdt = (xg * dt_e).astype(BF16)
    xw = (xg * w_e).astype(BF16)

    cbm = lax.dot_general(cb16, bb, (((1,), (1,)), ((), ())), preferred_element_type=F32)
    Hc = L // 2
    tril = (lax.broadcasted_iota(jnp.int32, (Hc, Hc), 1) <= lax.broadcasted_iota(jnp.int32, (Hc, Hc), 0))
    for r in range(R):
        a_col = ac8[:, r:r + 1]
        a_row = act8[r:r + 1, :]
        xr = xdt[:, r * P:(r + 1) * P]
        d00 = jnp.exp2(jnp.where(tril, a_col[:Hc] - a_row[:, :Hc], NEG_INF))
        d10 = jnp.exp2(a_col[Hc:] - a_row[:, :Hc])
        d11 = jnp.exp2(jnp.where(tril, a_col[Hc:] - a_row[:, Hc:], NEG_INF))
        m_top = (cbm[:Hc, :Hc] * d00).astype(BF16)
        m_bot = jnp.concatenate([cbm[Hc:, :Hc] * d10, cbm[Hc:, Hc:] * d11], axis=1).astype(BF16)
        yint_ref[:Hc, r * P:(r + 1) * P] = jnp.dot(m_top, xr[:Hc], preferred_element_type=F32)
        yint_ref[Hc:, r * P:(r + 1) * P] = jnp.dot(m_bot, xr, preferred_element_type=F32)

    state = state_ref[...]
    y_inter = jnp.dot(cb16, state.astype(BF16), preferred_element_type=F32) * eac_e
    bt = bm.T.astype(BF16)
    state_ref[...] = state * elast_e + jnp.dot(bt, xw, preferred_element_type=F32)

    y = yint_ref[...] + y_inter + dexp_ref[...] * xg
    gf = y * _silu(z_ref[...].astype(F32))
    o_ref[...] = _rms(gf, gw_ref[...], GATED_NORM_EPS).astype(o_ref.dtype)


def ssd_mixer(zx, conv_w, conv_b, pkg, acg, act, d_exp, gnorm_w):
    s = zx.shape[0]
    L = SSD_CHUNK
    G = SSM_GROUPS
    R = SSM_HEADS_PER_GROUP
    gn = SSM_STATE
    xoff = D_INNER // GROUP_CH
    boff = (2 * D_INNER) // gn
    coff = boff + G
    wboff = D_INNER // gn
    wcoff = wboff + G
    n_pk = N_SCALES * N_SPLIT * R
    pk_row = jnp.arange(n_pk)
    out_col = jnp.arange(N_SCALES * GROUP_CH)
    expand = ((pk_row[:, None] // (N_SPLIT * R) == out_col[None, :] // GROUP_CH)
              & (pk_row[:, None] % R == (out_col[None, :] % GROUP_CH) // SSM_HEAD_DIM)).astype(BF16)
    in_specs = [
        pl.BlockSpec((L, GROUP_CH), lambda g, c: (c, g)),
        pl.BlockSpec((L, GROUP_CH), lambda g, c: (c, xoff + g)),
        pl.BlockSpec((L, gn), lambda g, c: (c, boff + g)),
        pl.BlockSpec((L, gn), lambda g, c: (c, coff + g)),
        pl.BlockSpec((D_CONV, GROUP_CH), lambda g, c: (0, g)),
        pl.BlockSpec((D_CONV, gn), lambda g, c: (0, wboff + g)),
        pl.BlockSpec((D_CONV, gn), lambda g, c: (0, wcoff + g)),
        pl.BlockSpec((1, GROUP_CH), lambda g, c: (0, g)),
        pl.BlockSpec((1, gn), lambda g, c: (0, wboff + g)),
        pl.BlockSpec((1, gn), lambda g, c: (0, wcoff + g)),
        pl.BlockSpec((None, L, n_pk), lambda g, c: (g, c, 0)),
        pl.BlockSpec((None, L, R), lambda g, c: (g, c, 0)),
        pl.BlockSpec((None, R, L), lambda g, c: (g, 0, c)),
        pl.BlockSpec((1, GROUP_CH), lambda g, c: (0, g)),
        pl.BlockSpec((1, GROUP_CH), lambda g, c: (0, g)),
        pl.BlockSpec((n_pk, N_SCALES * GROUP_CH), lambda g, c: (0, 0)),
    ]
    return pl.pallas_call(
        _ssd_kernel,
        out_shape=jax.ShapeDtypeStruct((s, D_INNER), BF16),
        grid=(G, s // L),
        in_specs=in_specs,
        out_specs=pl.BlockSpec((L, GROUP_CH), lambda g, c: (c, g)),
        scratch_shapes=[
            pltpu.VMEM((L + SUBLANES, GROUP_CH), F32),
            pltpu.VMEM((L + SUBLANES, gn), F32),
            pltpu.VMEM((L + SUBLANES, gn), F32),
            pltpu.VMEM((gn, GROUP_CH), F32),
            pltpu.VMEM((L, GROUP_CH), F32),
        ],
        compiler_params=_cparams(("arbitrary", "arbitrary")),
        name="ssd",
    )(zx, zx, zx, zx, conv_w, conv_w, conv_w, conv_b, conv_b, conv_b,
      pkg, acg, act, d_exp, gnorm_w, expand)


def mamba_layer(h, norm_w, w_in, conv_w, conv_b, dt_bias, a_log, d_skip, gnorm_w, w_out):
    s = h.shape[0]
    n_zx = D_INNER + CONV_DIM
    w_zx = w_in[:, :n_zx].astype(BF16)
    w_dt = jnp.pad(w_in[:, n_zx:], ((0, 0), (0, LANES - SSM_HEADS))).astype(BF16)
    zx = rms_mm(h, norm_w, w_zx, BF16)
    dt_raw = rms_mm(h, norm_w, w_dt, F32)
    pad = lambda v: jnp.pad(v.astype(F32), (0, LANES - SSM_HEADS)).reshape(1, LANES)
    ac, pk = dt_prep(dt_raw, pad(dt_bias), pad(a_log))
    acg = ac[:, :SSM_HEADS].reshape(s, SSM_GROUPS, SSM_HEADS_PER_GROUP).transpose(1, 0, 2)
    act = acg.transpose(0, 2, 1)
    pkg = pk.reshape(s, N_SCALES * N_SPLIT, LANES)[:, :, :SSM_HEADS]
    pkg = pkg.reshape(s, N_SCALES * N_SPLIT, SSM_GROUPS, SSM_HEADS_PER_GROUP).transpose(2, 0, 1, 3)
    pkg = pkg.reshape(SSM_GROUPS, s, N_SCALES * N_SPLIT * SSM_HEADS_PER_GROUP)
    d_exp = jnp.repeat(d_skip.astype(F32), SSM_HEAD_DIM).reshape(1, D_INNER)
    y = ssd_mixer(zx, conv_w, conv_b.reshape(1, CONV_DIM), pkg, acg, act, d_exp,
                  gnorm_w.reshape(1, D_INNER))
    return mm_res(y, w_out.astype(BF16), h)


CUM_TILE = 256


def _cum_kernel(f_ref, b_ref, o_ref, carry_ref):
    @pl.when(pl.program_id(0) == 0)
    def _():
        carry_ref[...] = jnp.zeros_like(carry_ref)

    log_f = jax.nn.log_sigmoid(f_ref[...] + b_ref[...])
    cum = jnp.dot(_tril_ones(CUM_TILE), log_f, preferred_element_type=F32,
                  precision=lax.Precision.HIGHEST) + carry_ref[...]
    o_ref[...] = cum
    carry_ref[...] = cum[CUM_TILE - 1:CUM_TILE, :]


def fox_cum(f_raw, b_f):
    s, w = f_raw.shape
    return pl.pallas_call(
        _cum_kernel,
        out_shape=jax.ShapeDtypeStruct((s, w), F32),
        grid=(s // CUM_TILE,),
        in_specs=[pl.BlockSpec((CUM_TILE, w), lambda i: (i, 0)), pl.BlockSpec((1, w), lambda i: (0, 0))],
        out_specs=pl.BlockSpec((CUM_TILE, w), lambda i: (i, 0)),
        scratch_shapes=[pltpu.VMEM((1, w), F32)],
        compiler_params=_cparams(("arbitrary",)),
        name="fox_cum",
    )(f_raw, b_f)


FOX_TILE = 512


FOX_AUG = 2 * FOX_HEAD_DIM


def _fox_kernel(q_ref, k_ref, v_ref, cq_ref, ck_ref, o_ref, kaug, vt, qaug, za, zb, m_ref, l_ref, acc_ref):
    i = pl.program_id(1)
    T = FOX_TILE
    D = FOX_HEAD_DIM
    lane = lax.broadcasted_iota(jnp.int32, (1, D), 1)

    @pl.when(i == 0)
    def _():
        c2 = ck_ref[...] * (-LOG2E)
        hi = c2.astype(BF16)
        r1 = c2 - hi.astype(F32)
        mid = r1.astype(BF16)
        lo = (r1 - mid.astype(F32)).astype(BF16)
        aug = jnp.where(lane == 0, hi.astype(F32),
                        jnp.where(lane == 1, mid.astype(F32),
                                  jnp.where(lane == 2, lo.astype(F32), 0.0)))
        kaug[:, 0:D] = k_ref[...]
        kaug[:, D:FOX_AUG] = aug.astype(BF16)
        vt[...] = v_ref[...].astype(F32).T.astype(BF16)

    qaug[:, 0:D] = q_ref[...]
    qaug[:, D:FOX_AUG] = jnp.broadcast_to(jnp.where(lane < 3, 1.0, 0.0), (T, D)).astype(BF16)
    m_ref[...] = jnp.full_like(m_ref, NEG_INF)
    l_ref[...] = jnp.zeros_like(l_ref)
    acc_ref[...] = jnp.zeros_like(acc_ref)
    cq2 = cq_ref[...] * LOG2E
    nt = (((1,), (1,)), ((), ()))

    def logits(j, z_ref):
        off = pl.multiple_of(j * T, T)
        z_ref[...] = lax.dot_general(kaug[pl.ds(off, T), :], qaug[...], nt,
                                     preferred_element_type=F32)

    def softmax_pv(j, z_ref, masked):
        off = pl.multiple_of(j * T, T)
        z = z_ref[...]
        if masked:
            key = lax.broadcasted_iota(jnp.int32, (T, T), 0)
            qry = lax.broadcasted_iota(jnp.int32, (T, T), 1)
            z = jnp.where(key <= qry, z, NEG_INF)
        m_old = m_ref[...]
        m_new = jnp.maximum(m_old, jnp.max(z, axis=0, keepdims=True) + cq2)
        p = jnp.exp2(z + (cq2 - m_new))
        alpha = jnp.exp2(m_old - m_new)
        l_ref[...] = alpha * l_ref[...] + jnp.sum(p, axis=0, keepdims=True)
        acc_ref[...] = alpha * acc_ref[...] + jnp.dot(vt[:, pl.ds(off, T)], p.astype(BF16),
                                                      preferred_element_type=F32)
        m_ref[...] = m_new

    logits(0, za)

    def pair(pp, carry):
        j = 2 * pp
        logits(j + 1, zb)
        softmax_pv(j, za, False)
        logits(j + 2, za)
        softmax_pv(j + 1, zb, False)
        return carry

    lax.fori_loop(0, i // 2, pair, 0)

    @pl.when(i % 2 == 0)
    def _():
        softmax_pv(i, za, True)

    @pl.when(i % 2 == 1)
    def _():
        logits(i, zb)
        softmax_pv(i - 1, za, False)
        softmax_pv(i, zb, True)

    o_ref[...] = (acc_ref[...] / l_ref[...]).T.astype(o_ref.dtype)


def fox_attention(q, kv, cum):
    s = q.shape[0]
    T = FOX_TILE
    D = FOX_HEAD_DIM
    cq = cum[:, :FOX_HEADS].T.reshape(FOX_HEADS, 1, s)
    ck = cum[:, :FOX_HEADS].T.reshape(FOX_HEADS, s, 1)
    return pl.pallas_call(
        _fox_kernel,
        out_shape=jax.ShapeDtypeStruct((s, ATT_DIM), BF16),
        grid=(FOX_HEADS, s // T),
        in_specs=[
            pl.BlockSpec((T, D), lambda h, i: (i, h)),
            pl.BlockSpec((s, D), lambda h, i: (0, h)),
            pl.BlockSpec((s, D), lambda h, i: (0, FOX_HEADS + h)),
            pl.BlockSpec((None, 1, T), lambda h, i: (h, 0, i)),
            pl.BlockSpec((None, s, 1), lambda h, i: (h, 0, 0)),
        ],
        out_specs=pl.BlockSpec((T, D), lambda h, i: (i, h)),
        scratch_shapes=[pltpu.VMEM((s, FOX_AUG), BF16), pltpu.VMEM((D, s), BF16),
                        pltpu.VMEM((T, FOX_AUG), BF16),
                        pltpu.VMEM((T, T), F32), pltpu.VMEM((T, T), F32),
                        pltpu.VMEM((1, T), F32), pltpu.VMEM((1, T), F32), pltpu.VMEM((D, T), F32)],
        compiler_params=_cparams(("arbitrary", "arbitrary")),
        name="fox_attn",
    )(q, kv, kv, cq, ck)


def fox_shared_kv(h, kv_norm, w_kvf, b_f):
    w_kv = w_kvf[:, :2 * ATT_DIM].astype(BF16)
    w_f = jnp.pad(w_kvf[:, 2 * ATT_DIM:], ((0, 0), (0, LANES - FOX_HEADS))).astype(BF16)
    kv = rms_mm(h, kv_norm, w_kv, BF16)
    f_raw = rms_mm(h, kv_norm, w_f, F32)
    b_pad = jnp.pad(b_f.astype(F32), (0, LANES - FOX_HEADS)).reshape(1, LANES)
    return kv, fox_cum(f_raw, b_pad)


def fox_layer(h, norm_w, w_q, w_o, kv, cum):
    q = rms_mm(h, norm_w, w_q.astype(BF16), BF16, out_scale=FOX_SCALE * LOG2E)
    o = fox_attention(q, kv, cum)
    return mm_res(o, w_o.astype(BF16), h)


ROUTE_TILE = 256
N_TOP = PEER_TOPK + 1
TOP_ROWS = 24
_CAND_NB = [N_TOP // (a + 1) for a in range(1, SUBLANES)]
CAND_ROWS = TOP_ROWS + SUBLANES * (SUBLANES - 1) + (TOP_ROWS - SUBLANES)


def _sort_network(n):
    pairs = []
    p = 1
    while p < n:
        k = p
        while k >= 1:
            for j in range(k % p, n - k, 2 * k):
                for i in range(min(k, n - j - k)):
                    if (i + j) // (2 * p) == (i + j + k) // (2 * p):
                        pairs.append((i + j, i + j + k))
            k //= 2
        p *= 2
    return pairs


def _extract_top(x, out_ref, n):
    n_tiles = x.shape[0] // SUBLANES
    net = [(a, b) for a, b in _sort_network(1 << (n_tiles - 1).bit_length()) if b < n_tiles]
    for c0 in range(0, x.shape[1], LANES):
        lanes = slice(c0, c0 + LANES)
        tiles = [x[k * SUBLANES:(k + 1) * SUBLANES, lanes] for k in range(n_tiles)]
        for a, b in net:
            tiles[a], tiles[b] = jnp.maximum(tiles[a], tiles[b]), jnp.minimum(tiles[a], tiles[b])
        for k in range(n):
            m = jnp.max(tiles[0], axis=0, keepdims=True)
            out_ref[k:k + 1, lanes] = m
            taken = tiles[0] >= m
            depth = min(n_tiles, n - k)
            for i in range(depth):
                nxt = tiles[i + 1] if i + 1 < n_tiles else NEG_INF
                tiles[i] = jnp.where(taken, nxt, tiles[i])


def _route_kernel(q_ref, keys_ref, s1p_ref, s2_ref, taup_ref, v1_ref, v2_ref, ct_ref):
    T = ROUTE_TILE

    def head(h, carry):
        v1_ref[...] = jnp.full_like(v1_ref, NEG_INF)
        v2_ref[...] = jnp.full_like(v2_ref, NEG_INF)
        ct_ref[...] = jnp.full_like(ct_ref, NEG_INF)
        col = pl.multiple_of(h * 2 * PEER_HALF, 2 * PEER_HALF)
        q1 = q_ref[:, pl.ds(col, PEER_HALF)]
        q2 = q_ref[:, pl.ds(col + PEER_HALF, PEER_HALF)]
        nt = (((1,), (1,)), ((), ()))
        s1 = lax.dot_general(keys_ref[h, 0], q1, nt, preferred_element_type=F32)
        s2 = lax.dot_general(keys_ref[h, 1], q2, nt, preferred_element_type=F32)
        _extract_top(s1, v1_ref, N_TOP)
        _extract_top(s2, v2_ref, N_TOP)
        v1 = v1_ref[...]
        v2 = v2_ref[...]
        pieces = [v1[0:1, :] + v2]
        rows8 = lax.broadcasted_iota(jnp.int32, (SUBLANES, T), 0)
        for a in range(1, SUBLANES):
            pieces.append(v1[a:a + 1, :] + jnp.where(rows8 < _CAND_NB[a - 1], v2[0:SUBLANES, :], NEG_INF))
        pieces.append(v1[SUBLANES:TOP_ROWS, :] + v2[0:1, :])
        cand = jnp.concatenate(pieces, axis=0)
        _extract_top(cand, ct_ref, N_TOP)
        m = ct_ref[0:1, :]
        tau = 0.5 * (ct_ref[PEER_TOPK - 1:PEER_TOPK, :] + ct_ref[PEER_TOPK:PEER_TOPK + 1, :])
        zsum = jnp.sum(jnp.where(cand >= tau, jnp.exp(cand - m), 0.0), axis=0, keepdims=True)
        shift = m + jnp.log(zsum)
        s1p_ref[h] = (s1 - shift) * LOG2E
        s2_ref[h] = s2 * LOG2E
        taup_ref[pl.ds(h, 1), :] = (tau - shift) * LOG2E
        return carry

    lax.fori_loop(0, PEER_HEADS, head, 0)


def peer_route(q, keys):
    s = q.shape[0]
    T = ROUTE_TILE
    big = pl.BlockSpec((PEER_HEADS, PEER_N_KEYS, T), lambda t: (0, 0, t))
    return pl.pallas_call(
        _route_kernel,
        out_shape=[jax.ShapeDtypeStruct((PEER_HEADS, PEER_N_KEYS, s), F32)] * 2
        + [jax.ShapeDtypeStruct((PEER_HEADS, s), F32)],
        grid=(s // T,),
        in_specs=[
            pl.BlockSpec((T, PEER_HEADS * 2 * PEER_HALF), lambda t: (t, 0)),
            pl.BlockSpec((PEER_HEADS, 2, PEER_N_KEYS, PEER_HALF), lambda t: (0, 0, 0, 0)),
        ],
        out_specs=[big, big, pl.BlockSpec((PEER_HEADS, T), lambda t: (0, t))],
        scratch_shapes=[pltpu.VMEM((TOP_ROWS, T), F32), pltpu.VMEM((TOP_ROWS, T), F32),
                        pltpu.VMEM((TOP_ROWS, T), F32)],
        compiler_params=_cparams(("arbitrary",)),
        name="peer_route",
    )(q, keys)


PEER_TOK_TILE = 512
PEER_EXP_TILE = 512
PEER_EXP_SUB = 512
PEER_TOK_GROUP = 256
PEER_ACT_PARTS = 1
_SQRT_HALF = 0.7071067811865476


def _gelu(x):
    return 0.5 * x * (1.0 + lax.erf(x * _SQRT_HALF))


def _peer_kernel(xn_ref, u_ref, vt_ref, s1p_ref, s2_ref, taup_ref, o_ref,
                 at0, at1, ga, acc_ref, *, n_e):
    s = pl.program_id(0)
    T = PEER_TOK_TILE
    n_i = PEER_EXP_TILE // PEER_N_KEYS
    n_items = pl.num_programs(0) - 1
    e_out = jnp.clip(s - 1, 0, n_items - 1) % n_e

    @pl.when(s == 0)
    def _():
        at1[...] = jnp.zeros_like(at1)

    @pl.when(e_out == 0)
    def _():
        acc_ref[...] = jnp.zeros_like(acc_ref)

    HT = PEER_TOK_GROUP

    def gate_chunk(at_r, ga_w, lane0, ii, key0):
        lanes = pl.ds(lane0, LANES)
        rb = 2 * SUBLANES
        k1 = key0 + ii
        s1b = [jnp.broadcast_to(s1p_ref[h, k1:k1 + 1, lanes], (SUBLANES, LANES)) for h in range(PEER_HEADS)]
        tb = [jnp.broadcast_to(taup_ref[h:h + 1, lanes], (SUBLANES, LANES)) for h in range(PEER_HEADS)]
        for blk in range(PEER_N_KEYS // rb):
            parts = []
            for sub in range(rb // SUBLANES):
                keys = slice(blk * rb + sub * SUBLANES, blk * rb + (sub + 1) * SUBLANES)
                g = jnp.zeros((SUBLANES, LANES), F32)
                for h in range(PEER_HEADS):
                    ssum = s1b[h] + s2_ref[h, keys, lanes]
                    g = g + jnp.where(ssum >= tb[h], jnp.exp2(ssum), 0.0)
                parts.append(g)
            rows = slice(ii * PEER_N_KEYS + blk * rb, ii * PEER_N_KEYS + (blk + 1) * rb)
            g16 = jnp.concatenate(parts, axis=0)
            ga_w[rows, lanes] = (g16 * _gelu(at_r[rows, lanes])).astype(BF16)

    def step(at_w, at_r, key0):
        n_groups = T // HT
        act_cols = T // PEER_ACT_PARTS
        n_sub = PEER_EXP_SUB // PEER_N_KEYS
        for it in range(PEER_EXP_TILE // PEER_EXP_SUB):
            erows = slice(it * PEER_EXP_SUB, (it + 1) * PEER_EXP_SUB)
            for grp in range(n_groups):
                cols = slice(grp * HT, (grp + 1) * HT)
                for sub in range(HT // LANES):
                    for ii in range(n_sub):
                        gate_chunk(at_r, ga, grp * HT + sub * LANES, it * n_sub + ii, key0)
                acc_ref[:, cols] += jnp.dot(vt_ref[:, erows], ga[erows, cols], preferred_element_type=F32)
                if grp < PEER_ACT_PARTS:
                    acols = slice(grp * act_cols, (grp + 1) * act_cols)
                    at_w[erows, acols] = lax.dot_general(u_ref[erows, :], xn_ref[acols, :],
                                                         (((1,), (1,)), ((), ())), preferred_element_type=F32)

    @pl.when(s % 2 == 0)
    def _():
        step(at0, at1, n_i)

    @pl.when(s % 2 == 1)
    def _():
        step(at1, at0, 0)

    @pl.when((e_out == n_e - 1) & (s >= 1))
    def _():
        o_ref[...] = acc_ref[...].T


def peer_experts(xn, u, vt, s1p, s2, taup):
    s, d = xn.shape
    T = PEER_TOK_TILE
    Et = PEER_EXP_TILE
    n_i = Et // PEER_N_KEYS
    n_e = PEER_EXPERTS // Et
    n_items = (s // T) * n_e
    assert n_e % 2 == 0 and 2 * n_i == SUBLANES

    def item(step, lag):
        return jnp.clip(step - lag, 0, n_items - 1)

    tok = lambda step, lag: item(step, lag) // n_e
    exp = lambda step, lag: item(step, lag) % n_e
    return pl.pallas_call(
        functools.partial(_peer_kernel, n_e=n_e),
        out_shape=jax.ShapeDtypeStruct((s, d), F32),
        grid=(n_items + 1,),
        in_specs=[
            pl.BlockSpec((T, d), lambda i: (tok(i, 0), 0)),
            pl.BlockSpec((Et, d), lambda i: (exp(i, 0), 0)),
            pl.BlockSpec((d, Et), lambda i: (0, exp(i, 1))),
            pl.BlockSpec((PEER_HEADS, SUBLANES, T), lambda i: (0, exp(i, 1) // 2, tok(i, 1))),
            pl.BlockSpec((PEER_HEADS, PEER_N_KEYS, T), lambda i: (0, 0, tok(i, 1))),
            pl.BlockSpec((PEER_HEADS, T), lambda i: (0, tok(i, 1))),
        ],
        out_specs=pl.BlockSpec((T, d), lambda i: (tok(i, 1), 0)),
        scratch_shapes=[pltpu.VMEM((Et, T), F32), pltpu.VMEM((Et, T), F32),
                        pltpu.VMEM((Et, T), BF16), pltpu.VMEM((d, T), F32)],
        compiler_params=_cparams(("arbitrary",)),
        name="peer_experts",
    )(xn, u, vt, s1p, s2, taup)


def _tables_kernel(u_ref, v_ref, ub_ref, vt_ref):
    ub_ref[...] = u_ref[...].astype(BF16)
    vt_ref[...] = v_ref[...].T.astype(BF16)


def peer_tables(u_all, v_all, layer, *, te=512):
    _, n, d = u_all.shape
    tab = pl.BlockSpec((None, te, d), lambda i: (layer, i, 0))
    return pl.pallas_call(
        _tables_kernel,
        out_shape=[jax.ShapeDtypeStruct((n, d), BF16), jax.ShapeDtypeStruct((d, n), BF16)],
        grid=(n // te,),
        in_specs=[tab, tab],
        out_specs=[pl.BlockSpec((te, d), lambda i: (i, 0)), pl.BlockSpec((d, te), lambda i: (0, i))],
        compiler_params=_cparams(("arbitrary",)),
        name="peer_tables",
    )(u_all, v_all)


def peer_layer(h, norm_w, w_q, sub_keys, u_all, v_all, layer):
    q, xn = rms_mm(h, norm_w, w_q.astype(BF16), BF16, emit_xn=True)
    s1p, s2, taup = peer_route(q, sub_keys.astype(BF16))
    ub, vt = peer_tables(u_all, v_all, layer)
    return peer_experts(xn, ub, vt, s1p, s2, taup)


def _trunk(x, p, a_norm, a_in_proj, a_conv_w, a_conv_b, a_dt_bias, a_A_log, a_D, a_gnorm, a_out_proj,
           kv_norm, w_kvf, b_f, b_norm, b_wq, b_wo, c_norm, c_wq, c_subkeys, c_u, c_v,
           e_norm, e_wg, e_bg, e_wp, f_norm):
    depth = p.shape[0]
    n_a = a_norm.shape[0]
    h = x[0]
    shared = None
    for i in range(depth):
        if i < n_a:
            h = mamba_layer(h, a_norm[i], a_in_proj[i], a_conv_w[i], a_conv_b[i], a_dt_bias[i],
                            a_A_log[i], a_D[i], a_gnorm[i], a_out_proj[i])
        else:
            j = i - n_a
            h = fox_layer(h, b_norm[j], b_wq[j], b_wo[j], *shared)
        y = peer_layer(h, c_norm[i], c_wq[i], c_subkeys[i], c_u, c_v, i)
        h = ple_layer(h, y, e_norm[i], e_wg[i].astype(BF16), e_bg[i], p[i, 0].astype(BF16),
                      e_wp[i].astype(BF16))
        if i == n_a - 1:
            shared = fox_shared_kv(h, kv_norm, w_kvf, b_f)
    return final_norm(h, f_norm)[None]


def kernel(x, p, a_norm, a_in_proj, a_conv_w, a_conv_b, a_dt_bias, a_A_log, a_D, a_gnorm, a_out_proj, kv_norm, w_kvf, b_f, b_norm, b_wq, b_wo, c_norm, c_wq, c_subkeys, c_u, c_v, e_norm, e_wg, e_bg, e_wp, f_norm):
    assert x.shape[0] == 1
    return _trunk(x, p, a_norm, a_in_proj, a_conv_w, a_conv_b, a_dt_bias, a_A_log, a_D, a_gnorm,
                  a_out_proj, kv_norm, w_kvf, b_f, b_norm, b_wq, b_wo, c_norm, c_wq, c_subkeys,
                  c_u, c_v, e_norm, e_wg, e_bg, e_wp, f_norm)
```
